```python
import math
import jax, jax.numpy as jnp
from jax import lax
import numpy as np

D_MODEL = 1024
BATCH = 16
SEQ = 2048
DEPTH = 2

CHUNK = 64
N_MIXERS = 2
N_A = (DEPTH + 1) // 2
N_B = DEPTH // 2
CONV_W = 3
CONV_GROUPS = 16
HG_EXPAND = 128
HG_HEADS = D_MODEL // HG_EXPAND
HG_DV = D_MODEL // HG_HEADS
MEM_LEN = 256
X_HEADS = 4
X_HEAD_DIM = D_MODEL // X_HEADS
D_FF = ((8 * D_MODEL // 3 + 255) // 256) * 256
EPS = 1e-6

kernel_name = "hybrid_shortconv_hgrn2_memxattn"


def rmsnorm(x, g):
    xf = x.astype(jnp.float32)
    y = xf * lax.rsqrt(jnp.mean(xf * xf, axis=-1, keepdims=True) + EPS)
    return (y * g.astype(jnp.float32)).astype(x.dtype)


def short_conv_mixer(h, w_in, w_conv, w_out):
    s = h.shape[1]
    gb, gc, u = jnp.split(h @ w_in, 3, axis=-1)
    u = gc * u
    up = jnp.pad(u, ((0, 0), (CONV_W - 1, 0), (0, 0)))
    z = sum(w_conv[j] * up[:, j:j + s] for j in range(CONV_W))
    return (gb * z) @ w_out


def _gla_chunk_step(state, inp):
    q, k, v, logg = inp
    c = q.shape[2]
    b = jnp.cumsum(logg, axis=2)
    o_inter = jnp.einsum('bhtk,bhkv->bhtv', q * jnp.exp(b), state)
    tri = jnp.tril(jnp.ones((c, c), dtype=bool))
    dec = jnp.where(tri[None, None, :, :, None],
                    b[:, :, :, None, :] - b[:, :, None, :, :], -jnp.inf)
    attn = jnp.einsum('bhtk,bhsk,bhtsk->bhts', q, k, jnp.exp(dec))
    o_intra = jnp.einsum('bhts,bhsv->bhtv', attn, v)
    b_last = b[:, :, -1:, :]
    new_state = (jnp.exp(b_last[:, :, 0, :])[..., None] * state
                 + jnp.einsum('bhsk,bhsv->bhkv', k * jnp.exp(b_last - b), v))
    return new_state, o_inter + o_intra


def hgrn2_mixer(h, w_in, w_out, g_norm, lb):
    bsz, s, _ = h.shape
    q, f, i, og = jnp.split(h @ w_in, 4, axis=-1)
    q = jax.nn.silu(q.astype(jnp.float32))
    lbf = lb.astype(jnp.float32)
    g = lbf + (1.0 - lbf) * jax.nn.sigmoid(f.astype(jnp.float32))
    k = 1.0 - g
    logg = jnp.log(g)
    nc = s // CHUNK

    def to_chunks(t, d):
        return t.astype(jnp.float32).reshape(bsz, nc, CHUNK, HG_HEADS, d).transpose(1, 0, 3, 2, 4)

    xs = (to_chunks(q, HG_EXPAND), to_chunks(k, HG_EXPAND), to_chunks(i, HG_DV), to_chunks(logg, HG_EXPAND))
    s0 = jnp.zeros((bsz, HG_HEADS, HG_EXPAND, HG_DV), jnp.float32)
    _, o = lax.scan(_gla_chunk_step, s0, xs)
    o = o.transpose(1, 0, 3, 2, 4).reshape(bsz, s, HG_HEADS, HG_DV)
    o = rmsnorm(o, g_norm.reshape(HG_HEADS, HG_DV))
    o = o.reshape(bsz, s, D_MODEL).astype(h.dtype) * jax.nn.silu(og)
    return o @ w_out


def memory_cross_attn(h, mem_n, w_q, w_kv, w_o):
    bsz, s, _ = h.shape
    q = (h @ w_q).reshape(bsz, s, X_HEADS, X_HEAD_DIM)
    k, v = jnp.split(mem_n @ w_kv, 2, axis=-1)
    k = k.reshape(bsz, -1, X_HEADS, X_HEAD_DIM)
    v = v.reshape(bsz, -1, X_HEADS, X_HEAD_DIM)
    sc = jnp.einsum('bshd,bmhd->bhsm', q.astype(jnp.float32), k.astype(jnp.float32)) * (X_HEAD_DIM ** -0.5)
    p = jax.nn.softmax(sc, axis=-1).astype(h.dtype)
    o = jnp.einsum('bhsm,bmhd->bshd', p, v).reshape(bsz, s, D_MODEL)
    return o @ w_o


def swiglu(h, w_in, w_out):
    a, b = jnp.split(h @ w_in, 2, axis=-1)
    return (jax.nn.silu(a) * b) @ w_out


def setup_inputs(seed: int = 0) -> dict:
    key = jax.random.key(seed)
    ks = jax.random.split(key, 24)
    d = D_MODEL

    def w(k, shape, fan_in):
        return jax.random.normal(k, shape, jnp.float32) * fan_in ** -0.5

    def gain(k, shape):
        return 1.0 + 0.02 * jax.random.normal(k, shape, jnp.float32)

    return {
        "x": jax.random.normal(ks[0], (BATCH, SEQ, d), jnp.float32),
        "mem": jax.random.normal(ks[1], (BATCH, MEM_LEN, d), jnp.float32),
        "norm_mix": gain(ks[2], (DEPTH, d)),
        "conv_w_in": w(ks[3], (N_A, d, 3 * d), d),
        "conv_w": w(ks[4], (N_A, CONV_W, d), CONV_W),
        "conv_w_out": w(ks[5], (N_A, d, d), d),
        "hgrn_w_in": w(ks[6], (N_B, d, 4 * d), d),
        "hgrn_w_out": w(ks[7], (N_B, d, d), d),
        "hgrn_norm": gain(ks[8], (N_B, d)),
        "hgrn_lb": 0.5 * jax.random.normal(ks[9], (DEPTH, d), jnp.float32),
        "norm_xattn": gain(ks[10], (DEPTH, d)),
        "norm_mem": gain(ks[11], (DEPTH, d)),
        "xattn_w_q": w(ks[12], (DEPTH, d, d), d),
        "xattn_w_kv": w(ks[13], (DEPTH, d, 2 * d), d),
        "xattn_w_o": w(ks[14], (DEPTH, d, d), d),
        "norm_ffn": gain(ks[15], (DEPTH, d)),
        "ffn_w_in": w(ks[16], (DEPTH, d, 2 * D_FF), d),
        "ffn_w_out": w(ks[17], (DEPTH, D_FF, d), D_FF),
        "final_norm": gain(ks[18], (d,)),
    }


def reference(x, mem, norm_mix, conv_w_in, conv_w, conv_w_out, hgrn_w_in, hgrn_w_out, hgrn_norm, hgrn_lb,
              norm_xattn, norm_mem, xattn_w_q, xattn_w_kv, xattn_w_o, norm_ffn, ffn_w_in, ffn_w_out, final_norm):
    lb_all = jnp.cumsum(jax.nn.softmax(hgrn_lb.astype(jnp.float32), axis=0), axis=0)
    lb_all = lb_all - lb_all[0:1]
    for i in range(DEPTH):
        h = rmsnorm(x, norm_mix[i])
        if i % N_MIXERS == 0:
            j = i // N_MIXERS
            y = short_conv_mixer(h, conv_w_in[j], conv_w[j], conv_w_out[j])
        else:
            j = i // N_MIXERS
            y = hgrn2_mixer(h, hgrn_w_in[j], hgrn_w_out[j], hgrn_norm[j], lb_all[i])
        x = x + y
        h = rmsnorm(x, norm_xattn[i])
        x = x + memory_cross_attn(h, rmsnorm(mem, norm_mem[i]), xattn_w_q[i], xattn_w_kv[i], xattn_w_o[i])
        h = rmsnorm(x, norm_ffn[i])
        x = x + swiglu(h, ffn_w_in[i], ffn_w_out[i])
    return rmsnorm(x, final_norm)
```

```python
import functools

import jax
import jax.numpy as jnp
from jax import lax
from jax.experimental import pallas as pl
from jax.experimental.pallas import tpu as pltpu

EPS = 1e-6
CHUNK = 64
LOG2_CHUNK = 6
HG_HEADS = 8
HG_DK = 128
X_HEADS = 4
CONV_W = 3
SUBLANES = 8
VMEM_LIMIT_BYTES = 56 * 1024 * 1024

F32 = jnp.float32
BF16 = jnp.bfloat16


def _rmsnorm(x, g):
    ms = jnp.mean(x * x, axis=-1, keepdims=True)
    return x * lax.rsqrt(ms + EPS) * g


def _sigmoid(x):
    return 1.0 / (1.0 + jnp.exp(-x))


def _dot(a, b):
    return jnp.dot(a, b, preferred_element_type=F32)


def _dot_nt(a, b):
    return lax.dot_general(a, b, (((1,), (1,)), ((), ())), preferred_element_type=F32)


def _dot_tn(a, b):
    return lax.dot_general(a, b, (((0,), (0,)), ((), ())), preferred_element_type=F32)


def _params(n_axes):
    return pltpu.CompilerParams(dimension_semantics=("arbitrary",) * n_axes,
                                vmem_limit_bytes=VMEM_LIMIT_BYTES)


def _resident(shape):
    return pl.BlockSpec(shape, lambda *_: (0,) * len(shape), pipeline_mode=pl.Buffered(1))


def _kv_kernel(mem_ref, g_ref, w_ref, k_ref, v_ref):
    d = mem_ref.shape[-1]
    h = _rmsnorm(mem_ref[0], g_ref[...]).astype(BF16)
    kv = _dot(h, w_ref[...])
    k_ref[0] = kv[:, :d].astype(BF16)
    v_ref[0] = kv[:, d:].astype(BF16)


def _kv_proj(mem, g, w_kv):
    b, m, d = mem.shape
    out = jax.ShapeDtypeStruct((b, m, d), BF16)
    return pl.pallas_call(
        _kv_kernel,
        grid=(b,),
        in_specs=[pl.BlockSpec((1, m, d), lambda i: (i, 0, 0)),
                  _resident((1, d)),
                  _resident((d, 2 * d))],
        out_specs=[pl.BlockSpec((1, m, d), lambda i: (i, 0, 0)),
                   pl.BlockSpec((1, m, d), lambda i: (i, 0, 0))],
        out_shape=[out, out],
        compiler_params=_params(1),
        name="kv_proj",
    )(mem, g, w_kv)


def _conv_kernel(x_ref, g_ref, win_ref, cw_ref, wout_ref, o_ref, carry_ref, *, nc):
    tm, d = x_ref.shape[1], x_ref.shape[2]

    @pl.when(pl.program_id(1) == 0)
    def _():
        carry_ref[...] = jnp.zeros_like(carry_ref)

    x = x_ref[0]
    h = _rmsnorm(x, g_ref[...]).astype(BF16)
    acc = x
    for c in range(d // nc):
        cs = slice(c * nc, (c + 1) * nc)
        gb = _dot(h, win_ref[:, c * nc:(c + 1) * nc])
        gc = _dot(h, win_ref[:, d + c * nc:d + (c + 1) * nc])
        u = _dot(h, win_ref[:, 2 * d + c * nc:2 * d + (c + 1) * nc])
        u2 = gc * u
        ext = jnp.concatenate([carry_ref[:, cs], u2], axis=0)
        s1 = pltpu.roll(ext, 1, 0)[SUBLANES:]
        s2 = pltpu.roll(ext, 2, 0)[SUBLANES:]
        z = cw_ref[0:1, cs] * s2 + cw_ref[1:2, cs] * s1 + cw_ref[2:3, cs] * u2
        carry_ref[:, cs] = u2[tm - SUBLANES:, :]
        acc = acc + _dot((gb * z).astype(BF16), wout_ref[cs, :])
    o_ref[0] = acc


def _conv_mixer(x, g, w_in, cw, w_out, *, tm=512, nc=512):
    b, s, d = x.shape
    return pl.pallas_call(
        functools.partial(_conv_kernel, nc=nc),
        grid=(b, s // tm),
        in_specs=[pl.BlockSpec((1, tm, d), lambda i, j: (i, j, 0)),
                  _resident((1, d)),
                  _resident((d, 3 * d)),
                  _resident((CONV_W, d)),
                  _resident((d, d))],
        out_specs=pl.BlockSpec((1, tm, d), lambda i, j: (i, j, 0)),
        out_shape=jax.ShapeDtypeStruct((b, s, d), F32),
        scratch_shapes=[pltpu.VMEM((SUBLANES, d), F32)],
        compiler_params=_params(2),
        name="conv_mixer",
    )(x, g, w_in, cw, w_out)


def _xattn_kernel(x_ref, g_ref, wq_ref, k_ref, v_ref, wo_ref, o_ref):
    d = x_ref.shape[2]
    hd = d // X_HEADS
    x = x_ref[0]
    h = _rmsnorm(x, g_ref[...]).astype(BF16)
    q = (_dot(h, wq_ref[...]) * (hd ** -0.5)).astype(BF16)
    heads = []
    for n in range(X_HEADS):
        cs = slice(n * hd, (n + 1) * hd)
        sc = _dot_nt(q[:, cs], k_ref[0, :, cs])
        e = jnp.exp(sc - jnp.max(sc, axis=-1, keepdims=True))
        p = e / jnp.sum(e, axis=-1, keepdims=True)
        heads.append(_dot(p.astype(BF16), v_ref[0, :, cs]).astype(BF16))
    o = jnp.concatenate(heads, axis=-1)
    o_ref[0] = x + _dot(o, wo_ref[...])


def _xattn(x, g, w_q, k, v, w_o, *, tm=512):
    b, s, d = x.shape
    m = k.shape[1]
    return pl.pallas_call(
        _xattn_kernel,
        grid=(b, s // tm),
        in_specs=[pl.BlockSpec((1, tm, d), lambda i, j: (i, j, 0)),
                  _resident((1, d)),
                  _resident((d, d)),
                  pl.BlockSpec((1, m, d), lambda i, j: (i, 0, 0)),
                  pl.BlockSpec((1, m, d), lambda i, j: (i, 0, 0)),
                  _resident((d, d))],
        out_specs=pl.BlockSpec((1, tm, d), lambda i, j: (i, j, 0)),
        out_shape=jax.ShapeDtypeStruct((b, s, d), F32),
        compiler_params=_params(2),
        name="xattn",
    )(x, g, w_q, k, v, w_o)


def _ffn_kernel(x_ref, g_ref, win_ref, wout_ref, gf_ref, o_ref, *, nc, final_norm):
    d_ff = wout_ref.shape[0]
    x = x_ref[...]
    h = _rmsnorm(x, g_ref[...]).astype(BF16)
    acc = x
    for c in range(d_ff // nc):
        a = _dot(h, win_ref[:, c * nc:(c + 1) * nc])
        b = _dot(h, win_ref[:, d_ff + c * nc:d_ff + (c + 1) * nc])
        act = (a * _sigmoid(a)) * b
        acc = acc + _dot(act.astype(BF16), wout_ref[c * nc:(c + 1) * nc, :])
    if final_norm:
        acc = _rmsnorm(acc, gf_ref[...])
    o_ref[...] = acc


def _ffn(x2d, g, w_in, w_out, g_final, *, final_norm, tm=512, nc=256):
    t, d = x2d.shape
    d_ff = w_out.shape[0]
    return pl.pallas_call(
        functools.partial(_ffn_kernel, nc=nc, final_norm=final_norm),
        grid=(t // tm,),
        in_specs=[pl.BlockSpec((tm, d), lambda i: (i, 0)),
                  _resident((1, d)),
                  _resident((d, 2 * d_ff)),
                  _resident((d_ff, d)),
                  _resident((1, d))],
        out_specs=pl.BlockSpec((tm, d), lambda i: (i, 0)),
        out_shape=jax.ShapeDtypeStruct((t, d), F32),
        compiler_params=_params(1),
        name="ffn",
    )(x2d, g, w_in, w_out, g_final)


N_LEVELS = LOG2_CHUNK + 2


def _hgrn_kernel(x_ref, g_ref, win_ref, lb_ref, gn_ref, wout_ref, o_ref,
                 state_ref, ql_ref, kl_ref, v_ref, sg_ref, tot_ref, y_ref, *, layer):
    tm, d = x_ref.shape[1], x_ref.shape[2]

    @pl.when(pl.program_id(1) == 0)
    def _():
        state_ref[...] = jnp.zeros_like(state_ref)

    lbr = lb_ref[...]
    e = jnp.exp(lbr - jnp.max(lbr, axis=0, keepdims=True))
    sm = e / jnp.sum(e, axis=0, keepdims=True)
    cum = sm[0:1, :]
    for i in range(1, layer + 1):
        cum = cum + sm[i:i + 1, :]
    lb = cum - sm[0:1, :]

    x = x_ref[0]
    h = _rmsnorm(x, g_ref[...]).astype(BF16)
    qf = _dot(h, win_ref[:, 0:d])
    qf = qf * _sigmoid(qf)
    gate = lb + (1.0 - lb) * _sigmoid(_dot(h, win_ref[:, d:2 * d]))
    kf = 1.0 - gate
    v_ref[...] = _dot(h, win_ref[:, 2 * d:3 * d]).astype(BF16)
    og = _dot(h, win_ref[:, 3 * d:4 * d])
    sg_ref[...] = og * _sigmoid(og)

    ql_ref[0] = qf.astype(BF16)
    kl_ref[0] = kf.astype(BF16)
    ql_ref[1] = (qf * gate).astype(BF16)
    row = lax.broadcasted_iota(jnp.int32, (tm, d), 0)
    c = jnp.log(gate)
    tot = c
    for m in range(LOG2_CHUNK):
        half = 1 << m
        if m > 0:
            ql_ref[m + 1] = (qf * jnp.exp(c)).astype(BF16)
            kl_ref[m + 1] = (kf * jnp.exp(tot - c)).astype(BF16)
        second = ((row >> m) & 1) == 1
        tot_prev = pltpu.roll(tot, half, 0)
        tot_next = pltpu.roll(tot, tm - half, 0)
        c = c + jnp.where(second, tot_prev, 0.0)
        tot = tot + jnp.where(second, tot_prev, tot_next)
    ql_ref[N_LEVELS - 1] = (qf * jnp.exp(c)).astype(BF16)
    kl_ref[N_LEVELS - 1] = (kf * jnp.exp(tot - c)).astype(BF16)
    tot_ref[...] = tot

    t_io = lax.broadcasted_iota(jnp.int32, (CHUNK, CHUNK), 0)
    s_io = lax.broadcasted_iota(jnp.int32, (CHUNK, CHUNK), 1)
    masks = [t_io == s_io]
    for m in range(LOG2_CHUNK):
        masks.append(((t_io >> (m + 1)) == (s_io >> (m + 1)))
                     & (((t_io >> m) & 1) == 1) & (((s_io >> m) & 1) == 0))

    def chunk_body(ci, carry):
        rows = pl.ds(pl.multiple_of(ci * CHUNK, CHUNK), CHUNK)
        last = pl.ds(pl.multiple_of(ci * CHUNK + CHUNK - SUBLANES, SUBLANES), SUBLANES)
        for n in range(HG_HEADS):
            cs = slice(n * HG_DK, (n + 1) * HG_DK)
            attn = jnp.zeros((CHUNK, CHUNK), F32)
            for lv in range(N_LEVELS - 1):
                kv_lv = 0 if lv == 1 else lv
                a = _dot_nt(ql_ref[lv, rows, cs], kl_ref[kv_lv, rows, cs])
                attn = attn + jnp.where(masks[lv], a, 0.0)
            vch = v_ref[rows, cs]
            st = state_ref[n]
            o = _dot(attn.astype(BF16), vch) + _dot_nt(ql_ref[N_LEVELS - 1, rows, cs], st.astype(BF16))
            decay = jnp.exp(tot_ref[last, cs][SUBLANES - 1:SUBLANES, :])
            state_ref[n] = st * decay + _dot_tn(vch, kl_ref[N_LEVELS - 1, rows, cs])
            on = _rmsnorm(o, gn_ref[:, cs])
            y_ref[rows, cs] = (on * sg_ref[rows, cs]).astype(BF16)
        return carry

    lax.fori_loop(0, tm // CHUNK, chunk_body, 0)
    o_ref[0] = x + _dot(y_ref[...], wout_ref[...])


def _hgrn_mixer(x, g, w_in, lb_raw, g_norm, w_out, *, layer, tm=256):
    b, s, d = x.shape
    n_layers = lb_raw.shape[0]
    return pl.pallas_call(
        functools.partial(_hgrn_kernel, layer=layer),
        grid=(b, s // tm),
        in_specs=[pl.BlockSpec((1, tm, d), lambda i, j: (i, j, 0)),
                  _resident((1, d)),
                  _resident((d, 4 * d)),
                  _resident((n_layers, d)),
                  _resident((1, d)),
                  _resident((d, d))],
        out_specs=pl.BlockSpec((1, tm, d), lambda i, j: (i, j, 0)),
        out_shape=jax.ShapeDtypeStruct((b, s, d), F32),
        scratch_shapes=[pltpu.VMEM((HG_HEADS, HG_DK, HG_DK), F32),
                        pltpu.VMEM((N_LEVELS, tm, d), BF16),
                        pltpu.VMEM((N_LEVELS, tm, d), BF16),
                        pltpu.VMEM((tm, d), BF16),
                        pltpu.VMEM((tm, d), F32),
                        pltpu.VMEM((tm, d), F32),
                        pltpu.VMEM((tm, d), BF16)],
        compiler_params=_params(2),
        name="hgrn_mixer",
    )(x, g, w_in, lb_raw, g_norm, w_out)


def kernel(x, mem, norm_mix, conv_w_in, conv_w, conv_w_out, hgrn_w_in, hgrn_w_out, hgrn_norm, hgrn_lb,
           norm_xattn, norm_mem, xattn_w_q, xattn_w_kv, xattn_w_o, norm_ffn, ffn_w_in, ffn_w_out, final_norm):
    b, s, d = x.shape
    depth = norm_mix.shape[0]
    n_mixers = 2
    row = lambda v: v.reshape(1, d).astype(F32)
    bf = lambda w: w.astype(BF16)
    x = x.astype(F32)
    for i in range(depth):
        j = i // n_mixers
        if i % n_mixers == 0:
            x = _conv_mixer(x, row(norm_mix[i]), bf(conv_w_in[j]), conv_w[j].astype(F32), bf(conv_w_out[j]))
        else:
            x = _hgrn_mixer(x, row(norm_mix[i]), bf(hgrn_w_in[j]), hgrn_lb.astype(F32), row(hgrn_norm[j]),
                            bf(hgrn_w_out[j]), layer=i)
        k, v = _kv_proj(mem.astype(F32), row(norm_mem[i]), bf(xattn_w_kv[i]))
        x = _xattn(x, row(norm_xattn[i]), bf(xattn_w_q[i]), k, v, bf(xattn_w_o[i]))
        x = _ffn(x.reshape(b * s, d), row(norm_ffn[i]), bf(ffn_w_in[i]), bf(ffn_w_out[i]), row(final_norm),
                 final_norm=(i == depth - 1)).reshape(b, s, d)
    return x
```

```python
import functools

import jax
import jax.numpy as jnp
from jax import lax
from jax.experimental import pallas as pl
from jax.experimental.pallas import tpu as pltpu

EPS = 1e-6
CHUNK = 64
LOG2_CHUNK = 6
HG_HEADS = 8
HG_DK = 128
X_HEADS = 4
CONV_W = 3
SUBLANES = 8
VMEM_LIMIT_BYTES = 56 * 1024 * 1024

F32 = jnp.float32
BF16 = jnp.bfloat16


def _rmsnorm(x, g):
    ms = jnp.mean(x * x, axis=-1, keepdims=True)
    return x * lax.rsqrt(ms + EPS) * g


def _sigmoid(x):
    return 1.0 / (1.0 + jnp.exp(-x))


def _dot(a, b):
    return jnp.dot(a, b, preferred_element_type=F32)


def _dot_nt(a, b):
    return lax.dot_general(a, b, (((1,), (1,)), ((), ())), preferred_element_type=F32)


def _dot_tn(a, b):
    return lax.dot_general(a, b, (((0,), (0,)), ((), ())), preferred_element_type=F32)


def _params(n_axes):
    return pltpu.CompilerParams(dimension_semantics=("arbitrary",) * n_axes,
                                vmem_limit_bytes=VMEM_LIMIT_BYTES)


def _resident(shape):
    return pl.BlockSpec(shape, lambda *_: (0,) * len(shape), pipeline_mode=pl.Buffered(1))


def _kv_kernel(mem_ref, g_ref, w_ref, k_ref, v_ref):
    d = mem_ref.shape[-1]
    h = _rmsnorm(mem_ref[0], g_ref[...]).astype(BF16)
    kv = _dot(h, w_ref[...])
    k_ref[0] = kv[:, :d].astype(BF16)
    v_ref[0] = kv[:, d:].astype(BF16)


def _kv_proj(mem, g, w_kv):
    b, m, d = mem.shape
    out = jax.ShapeDtypeStruct((b, m, d), BF16)
    return pl.pallas_call(
        _kv_kernel,
        grid=(b,),
        in_specs=[pl.BlockSpec((1, m, d), lambda i: (i, 0, 0)),
                  _resident((1, d)),
                  _resident((d, 2 * d))],
        out_specs=[pl.BlockSpec((1, m, d), lambda i: (i, 0, 0)),
                   pl.BlockSpec((1, m, d), lambda i: (i, 0, 0))],
        out_shape=[out, out],
        compiler_params=_params(1),
        name="kv_proj",
    )(mem, g, w_kv)


def _conv_kernel(x_ref, g_ref, win_ref, cw_ref, wout_ref, o_ref, carry_ref, *, nc):
    tm, d = x_ref.shape[1], x_ref.shape[2]

    @pl.when(pl.program_id(1) == 0)
    def _():
        carry_ref[...] = jnp.zeros_like(carry_ref)

    x = x_ref[0]
    h = _rmsnorm(x, g_ref[...]).astype(BF16)
    acc = x
    for c in range(d // nc):
        cs = slice(c * nc, (c + 1) * nc)
        gb = _dot(h, win_ref[:, c * nc:(c + 1) * nc])
        gc = _dot(h, win_ref[:, d + c * nc:d + (c + 1) * nc])
        u = _dot(h, win_ref[:, 2 * d + c * nc:2 * d + (c + 1) * nc])
        u2 = gc * u
        ext = jnp.concatenate([carry_ref[:, cs], u2], axis=0)
        s1 = pltpu.roll(ext, 1, 0)[SUBLANES:]
        s2 = pltpu.roll(ext, 2, 0)[SUBLANES:]
        z = cw_ref[0:1, cs] * s2 + cw_ref[1:2, cs] * s1 + cw_ref[2:3, cs] * u2
        carry_ref[:, cs] = u2[tm - SUBLANES:, :]
        acc = acc + _dot((gb * z).astype(BF16), wout_ref[cs, :])
    o_ref[0] = acc


def _conv_mixer(x, g, w_in, cw, w_out, *, tm=512, nc=512):
    b, s, d = x.shape
    return pl.pallas_call(
        functools.partial(_conv_kernel, nc=nc),
        grid=(b, s // tm),
        in_specs=[pl.BlockSpec((1, tm, d), lambda i, j: (i, j, 0)),
                  _resident((1, d)),
                  _resident((d, 3 * d)),
                  _resident((CONV_W, d)),
                  _resident((d, d))],
        out_specs=pl.BlockSpec((1, tm, d), lambda i, j: (i, j, 0)),
        out_shape=jax.ShapeDtypeStruct((b, s, d), F32),
        scratch_shapes=[pltpu.VMEM((SUBLANES, d), F32)],
        compiler_params=_params(2),
        name="conv_mixer",
    )(x, g, w_in, cw, w_out)


def _xattn_kernel(x_ref, g_ref, wq_ref, k_ref, v_ref, wo_ref, o_ref):
    d = x_ref.shape[2]
    hd = d // X_HEADS
    x = x_ref[0]
    h = _rmsnorm(x, g_ref[...]).astype(BF16)
    q = (_dot(h, wq_ref[...]) * (hd ** -0.5)).astype(BF16)
    heads = []
    for n in range(X_HEADS):
        cs = slice(n * hd, (n + 1) * hd)
        sc = _dot_nt(q[:, cs], k_ref[0, :, cs])
        e = jnp.exp(sc - jnp.max(sc, axis=-1, keepdims=True))
        p = e / jnp.sum(e, axis=-1, keepdims=True)
        heads.append(_dot(p.astype(BF16), v_ref[0, :, cs]).astype(BF16))
    o = jnp.concatenate(heads, axis=-1)
    o_ref[0] = x + _dot(o, wo_ref[...])


def _xattn(x, g, w_q, k, v, w_o, *, tm=512):
    b, s, d = x.shape
    m = k.shape[1]
    return pl.pallas_call(
        _xattn_kernel,
        grid=(b, s // tm),
        in_specs=[pl.BlockSpec((1, tm, d), lambda i, j: (i, j, 0)),
                  _resident((1, d)),
                  _resident((d, d)),
                  pl.BlockSpec((1, m, d), lambda i, j: (i, 0, 0)),
                  pl.BlockSpec((1, m, d), lambda i, j: (i, 0, 0)),
                  _resident((d, d))],
        out_specs=pl.BlockSpec((1, tm, d), lambda i, j: (i, j, 0)),
        out_shape=jax.ShapeDtypeStruct((b, s, d), F32),
        compiler_params=_params(2),
        name="xattn",
    )(x, g, w_q, k, v, w_o)


def _ffn_kernel(x_ref, g_ref, win_ref, wout_ref, gf_ref, o_ref, *, nc, final_norm):
    d_ff = wout_ref.shape[0]
    x = x_ref[...]
    h = _rmsnorm(x, g_ref[...]).astype(BF16)
    acc = x
    for c in range(d_ff // nc):
        a = _dot(h, win_ref[:, c * nc:(c + 1) * nc])
        b = _dot(h, win_ref[:, d_ff + c * nc:d_ff + (c + 1) * nc])
        act = (a * _sigmoid(a)) * b
        acc = acc + _dot(act.astype(BF16), wout_ref[c * nc:(c + 1) * nc, :])
    if final_norm:
        acc = _rmsnorm(acc, gf_ref[...])
    o_ref[...] = acc


def _ffn(x2d, g, w_in, w_out, g_final, *, final_norm, tm=512, nc=256):
    t, d = x2d.shape
    d_ff = w_out.shape[0]
    return pl.pallas_call(
        functools.partial(_ffn_kernel, nc=nc, final_norm=final_norm),
        grid=(t // tm,),
        in_specs=[pl.BlockSpec((tm, d), lambda i: (i, 0)),
                  _resident((1, d)),
                  _resident((d, 2 * d_ff)),
                  _resident((d_ff, d)),
                  _resident((1, d))],
        out_specs=pl.BlockSpec((tm, d), lambda i: (i, 0)),
        out_shape=jax.ShapeDtypeStruct((t, d), F32),
        compiler_params=_params(1),
        name="ffn",
    )(x2d, g, w_in, w_out, g_final)


def _split_rows(a):
    return [a[SUBLANES * i:SUBLANES * (i + 1)] for i in range(a.shape[0] // SUBLANES)]


def _decay_levels(gv):
    nv = len(gv)
    sub = lax.broadcasted_iota(jnp.int32, gv[0].shape, 0)
    e, f, t = list(gv), [jnp.ones_like(gv[0])] * nv, list(gv)
    levels = []
    for m in range(LOG2_CHUNK):
        levels.append((list(e), list(f)))
        half = 1 << m
        if half < SUBLANES:
            second = ((sub >> m) & 1) == 1
            for i in range(nv):
                t_prev = pltpu.roll(t[i], half, 0)
                t_next = pltpu.roll(t[i], SUBLANES - half, 0)
                e[i] = jnp.where(second, e[i] * t_prev, e[i])
                f[i] = jnp.where(second, f[i], f[i] * t_next)
                t[i] = t[i] * jnp.where(second, t_prev, t_next)
        else:
            hb = half // SUBLANES
            for blk in range(0, nv, 2 * hb):
                t_first, t_second = t[blk], t[blk + hb]
                both = t_first * t_second
                for i in range(blk, blk + hb):
                    f[i] = f[i] * t_second
                    e[i + hb] = e[i + hb] * t_first
                    t[i] = both
                    t[i + hb] = both
    levels.append((list(e), list(f)))
    return levels, t[0]


def _hgrn_kernel(x_ref, g_ref, win_ref, lb_ref, gn_ref, wout_ref, o_ref,
                 state_ref, q_ref, gate_ref, v_ref, sg_ref, y_ref, *, layer):
    tm, d = x_ref.shape[1], x_ref.shape[2]

    @pl.when(pl.program_id(1) == 0)
    def _():
        state_ref[...] = jnp.zeros_like(state_ref)

    lbr = lb_ref[...]
    e = jnp.exp(lbr - jnp.max(lbr, axis=0, keepdims=True))
    sm = e / jnp.sum(e, axis=0, keepdims=True)
    cum = sm[0:1, :]
    for i in range(1, layer + 1):
        cum = cum + sm[i:i + 1, :]
    lb = cum - sm[0:1, :]

    x = x_ref[0]
    h = _rmsnorm(x, g_ref[...]).astype(BF16)
    qf = _dot(h, win_ref[:, 0:d])
    q_ref[...] = qf * _sigmoid(qf)
    gate_ref[...] = lb + (1.0 - lb) * _sigmoid(_dot(h, win_ref[:, d:2 * d]))
    v_ref[...] = _dot(h, win_ref[:, 2 * d:3 * d]).astype(BF16)
    og = _dot(h, win_ref[:, 3 * d:4 * d])
    sg_ref[...] = og * _sigmoid(og)

    t_io = lax.broadcasted_iota(jnp.int32, (CHUNK, CHUNK), 0)
    s_io = lax.broadcasted_iota(jnp.int32, (CHUNK, CHUNK), 1)
    diff = t_io ^ s_io
    pair_level = jnp.zeros((CHUNK, CHUNK), jnp.int32)
    for m in range(LOG2_CHUNK):
        pair_level = jnp.where((diff >> m) == 1, m + 1, pair_level)
    pair_level = jnp.where(s_io > t_io, -1, pair_level)

    def scaled(vals, factors):
        return jnp.concatenate([a * b for a, b in zip(vals, factors)], axis=0).astype(BF16)

    def chunk_body(ci, carry):
        rows = pl.ds(pl.multiple_of(ci * CHUNK, CHUNK), CHUNK)
        for n in range(HG_HEADS):
            cs = slice(n * HG_DK, (n + 1) * HG_DK)
            qv = _split_rows(q_ref[rows, cs])
            gv = _split_rows(gate_ref[rows, cs])
            kv = [1.0 - g for g in gv]
            levels, chunk_decay = _decay_levels(gv)
            k0 = jnp.concatenate(kv, axis=0).astype(BF16)
            attn = _dot_nt(jnp.concatenate(qv, axis=0).astype(BF16), k0)
            attn = jnp.where(pair_level == 0, attn, 0.0)
            for m in range(LOG2_CHUNK):
                e_m, f_m = levels[m]
                k_m = k0 if m == 0 else scaled(kv, f_m)
                attn = jnp.where(pair_level == m + 1, _dot_nt(scaled(qv, e_m), k_m), attn)
            e_c, f_c = levels[LOG2_CHUNK]
            vch = v_ref[rows, cs]
            st = state_ref[n]
            o = _dot(attn.astype(BF16), vch) + _dot_nt(scaled(qv, e_c), st.astype(BF16))
            state_ref[n] = st * chunk_decay[0:1, :] + _dot_tn(vch, scaled(kv, f_c))
            on = _rmsnorm(o, gn_ref[:, cs])
            y_ref[rows, cs] = (on * sg_ref[rows, cs]).astype(BF16)
        return carry

    lax.fori_loop(0, tm // CHUNK, chunk_body, 0)
    o_ref[0] = x + _dot(y_ref[...], wout_ref[...])


def _hgrn_mixer(x, g, w_in, lb_raw, g_norm, w_out, *, layer, tm=256):
    b, s, d = x.shape
    n_layers = lb_raw.shape[0]
    return pl.pallas_call(
        functools.partial(_hgrn_kernel, layer=layer),
        grid=(b, s // tm),
        in_specs=[pl.BlockSpec((1, tm, d), lambda i, j: (i, j, 0)),
                  _resident((1, d)),
                  _resident((d, 4 * d)),
                  _resident((n_layers, d)),
                  _resident((1, d)),
                  _resident((d, d))],
        out_specs=pl.BlockSpec((1, tm, d), lambda i, j: (i, j, 0)),
        out_shape=jax.ShapeDtypeStruct((b, s, d), F32),
        scratch_shapes=[pltpu.VMEM((HG_HEADS, HG_DK, HG_DK), F32),
                        pltpu.VMEM((tm, d), F32),
                        pltpu.VMEM((tm, d), F32),
                        pltpu.VMEM((tm, d), BF16),
                        pltpu.VMEM((tm, d), F32),
                        pltpu.VMEM((tm, d), BF16)],
        compiler_params=_params(2),
        name="hgrn_mixer",
    )(x, g, w_in, lb_raw, g_norm, w_out)


def kernel(x, mem, norm_mix, conv_w_in, conv_w, conv_w_out, hgrn_w_in, hgrn_w_out, hgrn_norm, hgrn_lb,
           norm_xattn, norm_mem, xattn_w_q, xattn_w_kv, xattn_w_o, norm_ffn, ffn_w_in, ffn_w_out, final_norm):
    b, s, d = x.shape
    depth = norm_mix.shape[0]
    n_mixers = 2
    row = lambda v: v.reshape(1, d).astype(F32)
    bf = lambda w: w.astype(BF16)
    x = x.astype(F32)
    for i in range(depth):
        j = i // n_mixers
        if i % n_mixers == 0:
            x = _conv_mixer(x, row(norm_mix[i]), bf(conv_w_in[j]), conv_w[j].astype(F32), bf(conv_w_out[j]))
        else:
            x = _hgrn_mixer(x, row(norm_mix[i]), bf(hgrn_w_in[j]), hgrn_lb.astype(F32), row(hgrn_norm[j]),
                            bf(hgrn_w_out[j]), layer=i)
        k, v = _kv_proj(mem.astype(F32), row(norm_mem[i]), bf(xattn_w_kv[i]))
        x = _xattn(x, row(norm_xattn[i]), bf(xattn_w_q[i]), k, v, bf(xattn_w_o[i]))
        x = _ffn(x.reshape(b * s, d), row(norm_ffn[i]), bf(ffn_w_in[i]), bf(ffn_w_out[i]), row(final_norm),
                 final_norm=(i == depth - 1)).reshape(b, s, d)
    return x
```

```python
import functools

import jax
import jax.numpy as jnp
from jax import lax
from jax.experimental import pallas as pl
from jax.experimental.pallas import tpu as pltpu

EPS = 1e-6
CHUNK = 64
LOG2_CHUNK = 6
HG_HEADS = 8
HG_DK = 128
X_HEADS = 4
X_SUBTILES = 2
CONV_W = 3
SUBLANES = 8
VMEM_LIMIT_BYTES = 56 * 1024 * 1024

F32 = jnp.float32
BF16 = jnp.bfloat16


def _rmsnorm(x, g):
    ms = jnp.mean(x * x, axis=-1, keepdims=True)
    return x * lax.rsqrt(ms + EPS) * g


def _sigmoid(x):
    return 1.0 / (1.0 + jnp.exp(-x))


def _dot(a, b):
    return jnp.dot(a, b, preferred_element_type=F32)


def _dot_nt(a, b):
    return lax.dot_general(a, b, (((1,), (1,)), ((), ())), preferred_element_type=F32)


def _dot_tn(a, b):
    return lax.dot_general(a, b, (((0,), (0,)), ((), ())), preferred_element_type=F32)


def _params(n_axes):
    return pltpu.CompilerParams(dimension_semantics=("arbitrary",) * n_axes,
                                vmem_limit_bytes=VMEM_LIMIT_BYTES)


def _resident(shape):
    return pl.BlockSpec(shape, lambda *_: (0,) * len(shape), pipeline_mode=pl.Buffered(1))


def _kv_kernel(mem_ref, g_ref, w_ref, k_ref, v_ref):
    d = mem_ref.shape[-1]
    h = _rmsnorm(mem_ref[0], g_ref[...]).astype(BF16)
    kv = _dot(h, w_ref[...])
    k_ref[0] = kv[:, :d].astype(BF16)
    v_ref[0] = kv[:, d:].astype(BF16)


def _kv_proj(mem, g, w_kv):
    b, m, d = mem.shape
    out = jax.ShapeDtypeStruct((b, m, d), BF16)
    return pl.pallas_call(
        _kv_kernel,
        grid=(b,),
        in_specs=[pl.BlockSpec((1, m, d), lambda i: (i, 0, 0)),
                  _resident((1, d)),
                  _resident((d, 2 * d))],
        out_specs=[pl.BlockSpec((1, m, d), lambda i: (i, 0, 0)),
                   pl.BlockSpec((1, m, d), lambda i: (i, 0, 0))],
        out_shape=[out, out],
        compiler_params=_params(1),
        name="kv_proj",
    )(mem, g, w_kv)


def _conv_kernel(x_ref, g_ref, win_ref, cw_ref, wout_ref, o_ref, carry_ref, *, nc):
    tm, d = x_ref.shape[1], x_ref.shape[2]

    @pl.when(pl.program_id(1) == 0)
    def _():
        carry_ref[...] = jnp.zeros_like(carry_ref)

    x = x_ref[0]
    h = _rmsnorm(x, g_ref[...]).astype(BF16)

    def project(c):
        return [_dot(h, win_ref[:, part * d + c * nc:part * d + (c + 1) * nc]) for part in range(3)]

    def gated_conv(c, gb, gc, u):
        cs = slice(c * nc, (c + 1) * nc)
        u2 = gc * u
        ext = jnp.concatenate([carry_ref[:, cs], u2], axis=0)
        s1 = pltpu.roll(ext, 1, 0)[SUBLANES:]
        s2 = pltpu.roll(ext, 2, 0)[SUBLANES:]
        z = cw_ref[0:1, cs] * s2 + cw_ref[1:2, cs] * s1 + cw_ref[2:3, cs] * u2
        carry_ref[:, cs] = u2[tm - SUBLANES:, :]
        return (gb * z).astype(BF16)

    n_chunks = d // nc
    acc = x
    proj = project(0)
    for c in range(n_chunks):
        nxt = project(c + 1) if c + 1 < n_chunks else None
        acc = acc + _dot(gated_conv(c, *proj), wout_ref[c * nc:(c + 1) * nc, :])
        proj = nxt
    o_ref[0] = acc


def _conv_mixer(x, g, w_in, cw, w_out, *, tm=512, nc=256):
    b, s, d = x.shape
    return pl.pallas_call(
        functools.partial(_conv_kernel, nc=nc),
        grid=(b, s // tm),
        in_specs=[pl.BlockSpec((1, tm, d), lambda i, j: (i, j, 0)),
                  _resident((1, d)),
                  _resident((d, 3 * d)),
                  _resident((CONV_W, d)),
                  _resident((d, d))],
        out_specs=pl.BlockSpec((1, tm, d), lambda i, j: (i, j, 0)),
        out_shape=jax.ShapeDtypeStruct((b, s, d), F32),
        scratch_shapes=[pltpu.VMEM((SUBLANES, d), F32)],
        compiler_params=_params(2),
        name="conv_mixer",
    )(x, g, w_in, cw, w_out)


def _xattn_kernel(x_ref, g_ref, wq_ref, k_ref, v_ref, wo_ref, o_ref):
    tm, d = x_ref.shape[1], x_ref.shape[2]
    hd = d // X_HEADS
    ts = tm // X_SUBTILES
    subtiles = [slice(r * ts, (r + 1) * ts) for r in range(X_SUBTILES)]
    heads = [slice(n * hd, (n + 1) * hd) for n in range(X_HEADS)]
    xs = [x_ref[0, rs, :] for rs in subtiles]
    qs = [(_dot(_rmsnorm(x, g_ref[...]).astype(BF16), wq_ref[...]) * (hd ** -0.5)).astype(BF16) for x in xs]
    scores = [[_dot_nt(q[:, cs], k_ref[0, :, cs]) for cs in heads] for q in qs]
    outs = []
    for sc_r in scores:
        o_heads = []
        for sc, cs in zip(sc_r, heads):
            e = jnp.exp(sc - jnp.max(sc, axis=-1, keepdims=True))
            p = e / jnp.sum(e, axis=-1, keepdims=True)
            o_heads.append(_dot(p.astype(BF16), v_ref[0, :, cs]).astype(BF16))
        outs.append(jnp.concatenate(o_heads, axis=-1))
    for rs, x, o in zip(subtiles, xs, outs):
        o_ref[0, rs, :] = x + _dot(o, wo_ref[...])


def _xattn(x, g, w_q, k, v, w_o, *, tm=512):
    b, s, d = x.shape
    m = k.shape[1]
    return pl.pallas_call(
        _xattn_kernel,
        grid=(b, s // tm),
        in_specs=[pl.BlockSpec((1, tm, d), lambda i, j: (i, j, 0)),
                  _resident((1, d)),
                  _resident((d, d)),
                  pl.BlockSpec((1, m, d), lambda i, j: (i, 0, 0)),
                  pl.BlockSpec((1, m, d), lambda i, j: (i, 0, 0)),
                  _resident((d, d))],
        out_specs=pl.BlockSpec((1, tm, d), lambda i, j: (i, j, 0)),
        out_shape=jax.ShapeDtypeStruct((b, s, d), F32),
        compiler_params=_params(2),
        name="xattn",
    )(x, g, w_q, k, v, w_o)


def _ffn_kernel(x_ref, g_ref, win_ref, wout_ref, gf_ref, o_ref, *, nc, final_norm):
    d_ff = wout_ref.shape[0]
    x = x_ref[...]
    h = _rmsnorm(x, g_ref[...]).astype(BF16)
    acc = x
    for c in range(d_ff // nc):
        a = _dot(h, win_ref[:, c * nc:(c + 1) * nc])
        b = _dot(h, win_ref[:, d_ff + c * nc:d_ff + (c + 1) * nc])
        act = (a * _sigmoid(a)) * b
        acc = acc + _dot(act.astype(BF16), wout_ref[c * nc:(c + 1) * nc, :])
    if final_norm:
        acc = _rmsnorm(acc, gf_ref[...])
    o_ref[...] = acc


def _ffn(x2d, g, w_in, w_out, g_final, *, final_norm, tm=512, nc=256):
    t, d = x2d.shape
    d_ff = w_out.shape[0]
    return pl.pallas_call(
        functools.partial(_ffn_kernel, nc=nc, final_norm=final_norm),
        grid=(t // tm,),
        in_specs=[pl.BlockSpec((tm, d), lambda i: (i, 0)),
                  _resident((1, d)),
                  _resident((d, 2 * d_ff)),
                  _resident((d_ff, d)),
                  _resident((1, d))],
        out_specs=pl.BlockSpec((tm, d), lambda i: (i, 0)),
        out_shape=jax.ShapeDtypeStruct((t, d), F32),
        compiler_params=_params(1),
        name="ffn",
    )(x2d, g, w_in, w_out, g_final)


SCORE_LOOKAHEAD = 2


def _split_rows(a):
    return [a[SUBLANES * i:SUBLANES * (i + 1)] for i in range(a.shape[0] // SUBLANES)]


def _decay_levels(gv):
    nv = len(gv)
    sub = lax.broadcasted_iota(jnp.int32, gv[0].shape, 0)
    e, f, t = list(gv), [jnp.ones_like(gv[0])] * nv, list(gv)
    levels = []
    for m in range(LOG2_CHUNK):
        levels.append((list(e), list(f)))
        half = 1 << m
        if half < SUBLANES:
            second = ((sub >> m) & 1) == 1
            for i in range(nv):
                t_prev = pltpu.roll(t[i], half, 0)
                t_next = pltpu.roll(t[i], SUBLANES - half, 0)
                e[i] = jnp.where(second, e[i] * t_prev, e[i])
                f[i] = jnp.where(second, f[i], f[i] * t_next)
                t[i] = t[i] * jnp.where(second, t_prev, t_next)
        else:
            hb = half // SUBLANES
            for blk in range(0, nv, 2 * hb):
                t_first, t_second = t[blk], t[blk + hb]
                both = t_first * t_second
                for i in range(blk, blk + hb):
                    f[i] = f[i] * t_second
                    e[i + hb] = e[i + hb] * t_first
                    t[i] = both
                    t[i + hb] = both
    levels.append((list(e), list(f)))
    return levels, t[0]


def _hgrn_kernel(x_ref, g_ref, win_ref, lb_ref, gn_ref, wout_ref, o_ref,
                 state_ref, q_ref, gate_ref, v_ref, sg_ref, y_ref, *, layer):
    tm, d = x_ref.shape[1], x_ref.shape[2]

    @pl.when(pl.program_id(1) == 0)
    def _():
        state_ref[...] = jnp.zeros_like(state_ref)

    lbr = lb_ref[...]
    e = jnp.exp(lbr - jnp.max(lbr, axis=0, keepdims=True))
    sm = e / jnp.sum(e, axis=0, keepdims=True)
    cum = sm[0:1, :]
    for i in range(1, layer + 1):
        cum = cum + sm[i:i + 1, :]
    lb = cum - sm[0:1, :]

    x = x_ref[0]
    h = _rmsnorm(x, g_ref[...]).astype(BF16)
    qf = _dot(h, win_ref[:, 0:d])
    q_ref[...] = qf * _sigmoid(qf)
    gate_ref[...] = lb + (1.0 - lb) * _sigmoid(_dot(h, win_ref[:, d:2 * d]))
    v_ref[...] = _dot(h, win_ref[:, 2 * d:3 * d]).astype(BF16)
    og = _dot(h, win_ref[:, 3 * d:4 * d])
    sg_ref[...] = og * _sigmoid(og)

    t_io = lax.broadcasted_iota(jnp.int32, (CHUNK, CHUNK), 0)
    s_io = lax.broadcasted_iota(jnp.int32, (CHUNK, CHUNK), 1)
    diff = t_io ^ s_io
    pair_level = jnp.zeros((CHUNK, CHUNK), jnp.int32)
    for m in range(LOG2_CHUNK):
        pair_level = jnp.where((diff >> m) == 1, m + 1, pair_level)
    pair_level = jnp.where(s_io > t_io, -1, pair_level)

    def scaled(vals, factors):
        return jnp.concatenate([a * b for a, b in zip(vals, factors)], axis=0).astype(BF16)

    def scores(ci, n):
        rows = slice(ci * CHUNK, (ci + 1) * CHUNK)
        cs = slice(n * HG_DK, (n + 1) * HG_DK)
        qv = _split_rows(q_ref[rows, cs])
        gv = _split_rows(gate_ref[rows, cs])
        kv = [1.0 - g for g in gv]
        levels, chunk_decay = _decay_levels(gv)
        k0 = jnp.concatenate(kv, axis=0).astype(BF16)
        attn = _dot_nt(jnp.concatenate(qv, axis=0).astype(BF16), k0)
        attn = jnp.where(pair_level == 0, attn, 0.0)
        for m in range(LOG2_CHUNK):
            e_m, f_m = levels[m]
            k_m = k0 if m == 0 else scaled(kv, f_m)
            attn = jnp.where(pair_level == m + 1, _dot_nt(scaled(qv, e_m), k_m), attn)
        e_c, f_c = levels[LOG2_CHUNK]
        return attn.astype(BF16), scaled(qv, e_c), scaled(kv, f_c), chunk_decay[0:1, :]

    def recur(ci, n, attn, q_c, k_c, decay):
        rows = slice(ci * CHUNK, (ci + 1) * CHUNK)
        cs = slice(n * HG_DK, (n + 1) * HG_DK)
        vch = v_ref[rows, cs]
        st = state_ref[n]
        o = _dot(attn, vch) + _dot_nt(q_c, st.astype(BF16))
        state_ref[n] = st * decay + _dot_tn(vch, k_c)
        on = _rmsnorm(o, gn_ref[:, cs])
        y_ref[rows, cs] = (on * sg_ref[rows, cs]).astype(BF16)

    pending = []
    for ci in range(tm // CHUNK):
        for n in range(HG_HEADS):
            pending.append((ci, n, scores(ci, n)))
            if len(pending) > SCORE_LOOKAHEAD:
                pci, pn, vals = pending.pop(0)
                recur(pci, pn, *vals)
    for pci, pn, vals in pending:
        recur(pci, pn, *vals)
    o_ref[0] = x + _dot(y_ref[...], wout_ref[...])


def _hgrn_mixer(x, g, w_in, lb_raw, g_norm, w_out, *, layer, tm=256):
    b, s, d = x.shape
    n_layers = lb_raw.shape[0]
    return pl.pallas_call(
        functools.partial(_hgrn_kernel, layer=layer),
        grid=(b, s // tm),
        in_specs=[pl.BlockSpec((1, tm, d), lambda i, j: (i, j, 0)),
                  _resident((1, d)),
                  _resident((d, 4 * d)),
                  _resident((n_layers, d)),
                  _resident((1, d)),
                  _resident((d, d))],
        out_specs=pl.BlockSpec((1, tm, d), lambda i, j: (i, j, 0)),
        out_shape=jax.ShapeDtypeStruct((b, s, d), F32),
        scratch_shapes=[pltpu.VMEM((HG_HEADS, HG_DK, HG_DK), F32),
                        pltpu.VMEM((tm, d), F32),
                        pltpu.VMEM((tm, d), F32),
                        pltpu.VMEM((tm, d), BF16),
                        pltpu.VMEM((tm, d), F32),
                        pltpu.VMEM((tm, d), BF16)],
        compiler_params=_params(2),
        name="hgrn_mixer",
    )(x, g, w_in, lb_raw, g_norm, w_out)


def kernel(x, mem, norm_mix, conv_w_in, conv_w, conv_w_out, hgrn_w_in, hgrn_w_out, hgrn_norm, hgrn_lb,
           norm_xattn, norm_mem, xattn_w_q, xattn_w_kv, xattn_w_o, norm_ffn, ffn_w_in, ffn_w_out, final_norm):
    b, s, d = x.shape
    depth = norm_mix.shape[0]
    n_mixers = 2
    row = lambda v: v.reshape(1, d).astype(F32)
    bf = lambda w: w.astype(BF16)
    x = x.astype(F32)
    for i in range(depth):
        j = i // n_mixers
        if i % n_mixers == 0:
            x = _conv_mixer(x, row(norm_mix[i]), bf(conv_w_in[j]), conv_w[j].astype(F32), bf(conv_w_out[j]))
        else:
            x = _hgrn_mixer(x, row(norm_mix[i]), bf(hgrn_w_in[j]), hgrn_lb.astype(F32), row(hgrn_norm[j]),
                            bf(hgrn_w_out[j]), layer=i)
        k, v = _kv_proj(mem.astype(F32), row(norm_mem[i]), bf(xattn_w_kv[i]))
        x = _xattn(x, row(norm_xattn[i]), bf(xattn_w_q[i]), k, v, bf(xattn_w_o[i]))
        x = _ffn(x.reshape(b * s, d), row(norm_ffn[i]), bf(ffn_w_in[i]), bf(ffn_w_out[i]), row(final_norm),
                 final_norm=(i == depth - 1)).reshape(b, s, d)
    return x
```

```python
import functools

import jax
import jax.numpy as jnp
from jax import lax
from jax.experimental import pallas as pl
from jax.experimental.pallas import tpu as pltpu

EPS = 1e-6
CHUNK = 64
LOG2_CHUNK = 6
HG_HEADS = 8
HG_DK = 128
X_HEADS = 4
X_SUBTILES = 2
CONV_W = 3
SUBLANES = 8
VMEM_LIMIT_BYTES = 56 * 1024 * 1024

F32 = jnp.float32
BF16 = jnp.bfloat16


def _rmsnorm(x, g):
    ms = jnp.mean(x * x, axis=-1, keepdims=True)
    return x * lax.rsqrt(ms + EPS) * g


def _sigmoid(x):
    return 1.0 / (1.0 + jnp.exp(-x))


def _dot(a, b):
    return jnp.dot(a.astype(BF16), b.astype(BF16), preferred_element_type=F32)


def _dot_nt(a, b):
    return lax.dot_general(a.astype(BF16), b.astype(BF16), (((1,), (1,)), ((), ())), preferred_element_type=F32)


def _dot_tn(a, b):
    return lax.dot_general(a.astype(BF16), b.astype(BF16), (((0,), (0,)), ((), ())), preferred_element_type=F32)


def _params(n_axes):
    return pltpu.CompilerParams(dimension_semantics=("arbitrary",) * n_axes,
                                vmem_limit_bytes=VMEM_LIMIT_BYTES)


def _resident(shape):
    return pl.BlockSpec(shape, lambda *_: (0,) * len(shape), pipeline_mode=pl.Buffered(1))


def _layer(stacked, layer):
    shape = stacked.shape[1:]
    return pl.BlockSpec((None,) + shape, lambda *_: (layer,) + (0,) * len(shape), pipeline_mode=pl.Buffered(1))


def _kv_kernel(mem_ref, g_ref, w_ref, k_ref, v_ref):
    d = mem_ref.shape[-1]
    h = _rmsnorm(mem_ref[0], g_ref[...]).astype(BF16)
    kv = _dot(h, w_ref[...])
    k_ref[0] = kv[:, :d].astype(BF16)
    v_ref[0] = kv[:, d:].astype(BF16)


def _kv_proj(mem, g, w_kv, *, layer):
    b, m, d = mem.shape
    out = jax.ShapeDtypeStruct((b, m, d), BF16)
    return pl.pallas_call(
        _kv_kernel,
        grid=(b,),
        in_specs=[pl.BlockSpec((1, m, d), lambda i: (i, 0, 0)),
                  _layer(g, layer),
                  _layer(w_kv, layer)],
        out_specs=[pl.BlockSpec((1, m, d), lambda i: (i, 0, 0)),
                   pl.BlockSpec((1, m, d), lambda i: (i, 0, 0))],
        out_shape=[out, out],
        compiler_params=_params(1),
        name="kv_proj",
    )(mem, g, w_kv)


def _conv_kernel(x_ref, g_ref, win_ref, cw_ref, wout_ref, o_ref, carry_ref, *, nc):
    tm, d = x_ref.shape[1], x_ref.shape[2]

    @pl.when(pl.program_id(1) == 0)
    def _():
        carry_ref[...] = jnp.zeros_like(carry_ref)

    x = x_ref[0]
    h = _rmsnorm(x, g_ref[...]).astype(BF16)

    def project(c):
        return [_dot(h, win_ref[:, part * d + c * nc:part * d + (c + 1) * nc]) for part in range(3)]

    def gated_conv(c, gb, gc, u):
        cs = slice(c * nc, (c + 1) * nc)
        u2 = gc * u
        ext = jnp.concatenate([carry_ref[:, cs], u2], axis=0)
        s1 = pltpu.roll(ext, 1, 0)[SUBLANES:]
        s2 = pltpu.roll(ext, 2, 0)[SUBLANES:]
        z = cw_ref[0:1, cs] * s2 + cw_ref[1:2, cs] * s1 + cw_ref[2:3, cs] * u2
        carry_ref[:, cs] = u2[tm - SUBLANES:, :]
        return (gb * z).astype(BF16)

    n_chunks = d // nc
    acc = x
    proj = project(0)
    for c in range(n_chunks):
        nxt = project(c + 1) if c + 1 < n_chunks else None
        acc = acc + _dot(gated_conv(c, *proj), wout_ref[c * nc:(c + 1) * nc, :])
        proj = nxt
    o_ref[0] = acc


def _conv_mixer(x, g, w_in, cw, w_out, *, layer, mixer, tm=512, nc=256):
    b, s, d = x.shape
    return pl.pallas_call(
        functools.partial(_conv_kernel, nc=nc),
        grid=(b, s // tm),
        in_specs=[pl.BlockSpec((1, tm, d), lambda i, j: (i, j, 0)),
                  _layer(g, layer),
                  _layer(w_in, mixer),
                  _layer(cw, mixer),
                  _layer(w_out, mixer)],
        out_specs=pl.BlockSpec((1, tm, d), lambda i, j: (i, j, 0)),
        out_shape=jax.ShapeDtypeStruct((b, s, d), F32),
        scratch_shapes=[pltpu.VMEM((SUBLANES, d), F32)],
        compiler_params=_params(2),
        name="conv_mixer",
    )(x, g, w_in, cw, w_out)


def _xattn_kernel(x_ref, g_ref, wq_ref, k_ref, v_ref, wo_ref, o_ref):
    tm, d = x_ref.shape[1], x_ref.shape[2]
    hd = d // X_HEADS
    ts = tm // X_SUBTILES
    subtiles = [slice(r * ts, (r + 1) * ts) for r in range(X_SUBTILES)]
    heads = [slice(n * hd, (n + 1) * hd) for n in range(X_HEADS)]
    xs = [x_ref[0, rs, :] for rs in subtiles]
    qs = [(_dot(_rmsnorm(x, g_ref[...]).astype(BF16), wq_ref[...]) * (hd ** -0.5)).astype(BF16) for x in xs]
    scores = [[_dot_nt(q[:, cs], k_ref[0, :, cs]) for cs in heads] for q in qs]
    outs = []
    for sc_r in scores:
        o_heads = []
        for sc, cs in zip(sc_r, heads):
            e = jnp.exp(sc - jnp.max(sc, axis=-1, keepdims=True))
            p = e / jnp.sum(e, axis=-1, keepdims=True)
            o_heads.append(_dot(p.astype(BF16), v_ref[0, :, cs]).astype(BF16))
        outs.append(jnp.concatenate(o_heads, axis=-1))
    for rs, x, o in zip(subtiles, xs, outs):
        o_ref[0, rs, :] = x + _dot(o, wo_ref[...])


def _xattn(x, g, w_q, k, v, w_o, *, layer, tm=512):
    b, s, d = x.shape
    m = k.shape[1]
    return pl.pallas_call(
        _xattn_kernel,
        grid=(b, s // tm),
        in_specs=[pl.BlockSpec((1, tm, d), lambda i, j: (i, j, 0)),
                  _layer(g, layer),
                  _layer(w_q, layer),
                  pl.BlockSpec((1, m, d), lambda i, j: (i, 0, 0)),
                  pl.BlockSpec((1, m, d), lambda i, j: (i, 0, 0)),
                  _layer(w_o, layer)],
        out_specs=pl.BlockSpec((1, tm, d), lambda i, j: (i, j, 0)),
        out_shape=jax.ShapeDtypeStruct((b, s, d), F32),
        compiler_params=_params(2),
        name="xattn",
    )(x, g, w_q, k, v, w_o)


def _ffn_kernel(x_ref, g_ref, win_ref, wout_ref, gf_ref, o_ref, *, nc, final_norm):
    d_ff = wout_ref.shape[0]
    x = x_ref[...]
    h = _rmsnorm(x, g_ref[...]).astype(BF16)

    def up(c):
        return (_dot(h, win_ref[:, c * nc:(c + 1) * nc]),
                _dot(h, win_ref[:, d_ff + c * nc:d_ff + (c + 1) * nc]))

    n_chunks = d_ff // nc
    acc = x
    ab = up(0)
    for c in range(n_chunks):
        nxt = up(c + 1) if c + 1 < n_chunks else None
        a, b = ab
        act = (a * _sigmoid(a)) * b
        acc = acc + _dot(act, wout_ref[c * nc:(c + 1) * nc, :])
        ab = nxt
    if final_norm:
        acc = _rmsnorm(acc, gf_ref[...])
    o_ref[...] = acc


def _ffn(x2d, g, w_in, w_out, g_final, *, layer, final_norm, tm=512, nc=256):
    t, d = x2d.shape
    return pl.pallas_call(
        functools.partial(_ffn_kernel, nc=nc, final_norm=final_norm),
        grid=(t // tm,),
        in_specs=[pl.BlockSpec((tm, d), lambda i: (i, 0)),
                  _layer(g, layer),
                  _layer(w_in, layer),
                  _layer(w_out, layer),
                  _resident((1, d))],
        out_specs=pl.BlockSpec((tm, d), lambda i: (i, 0)),
        out_shape=jax.ShapeDtypeStruct((t, d), F32),
        compiler_params=_params(1),
        name="ffn",
    )(x2d, g, w_in, w_out, g_final)


SCORE_LOOKAHEAD = 2


def _split_rows(a):
    return [a[SUBLANES * i:SUBLANES * (i + 1)] for i in range(a.shape[0] // SUBLANES)]


def _decay_levels(gv):
    nv = len(gv)
    sub = lax.broadcasted_iota(jnp.int32, gv[0].shape, 0)
    e, f, t = list(gv), [jnp.ones_like(gv[0])] * nv, list(gv)
    levels = []
    for m in range(LOG2_CHUNK):
        levels.append((list(e), list(f)))
        half = 1 << m
        if half < SUBLANES:
            second = ((sub >> m) & 1) == 1
            for i in range(nv):
                t_prev = pltpu.roll(t[i], half, 0)
                t_next = pltpu.roll(t[i], SUBLANES - half, 0)
                e[i] = jnp.where(second, e[i] * t_prev, e[i])
                f[i] = jnp.where(second, f[i], f[i] * t_next)
                t[i] = t[i] * jnp.where(second, t_prev, t_next)
        else:
            hb = half // SUBLANES
            for blk in range(0, nv, 2 * hb):
                t_first, t_second = t[blk], t[blk + hb]
                both = t_first * t_second
                for i in range(blk, blk + hb):
                    f[i] = f[i] * t_second
                    e[i + hb] = e[i + hb] * t_first
                    t[i] = both
                    t[i + hb] = both
    levels.append((list(e), list(f)))
    return levels, t[0]


def _hgrn_kernel(x_ref, g_ref, win_ref, lb_ref, gn_ref, wout_ref, o_ref,
                 state_ref, q_ref, gate_ref, v_ref, sg_ref, y_ref, *, layer):
    tm, d = x_ref.shape[1], x_ref.shape[2]

    @pl.when(pl.program_id(1) == 0)
    def _():
        state_ref[...] = jnp.zeros_like(state_ref)

    lbr = lb_ref[...]
    e = jnp.exp(lbr - jnp.max(lbr, axis=0, keepdims=True))
    sm = e / jnp.sum(e, axis=0, keepdims=True)
    cum = sm[0:1, :]
    for i in range(1, layer + 1):
        cum = cum + sm[i:i + 1, :]
    lb = cum - sm[0:1, :]

    x = x_ref[0]
    h = _rmsnorm(x, g_ref[...]).astype(BF16)
    qf = _dot(h, win_ref[:, 0:d])
    q_ref[...] = qf * _sigmoid(qf)
    gate_ref[...] = lb + (1.0 - lb) * _sigmoid(_dot(h, win_ref[:, d:2 * d]))
    v_ref[...] = _dot(h, win_ref[:, 2 * d:3 * d]).astype(BF16)
    og = _dot(h, win_ref[:, 3 * d:4 * d])
    sg_ref[...] = og * _sigmoid(og)

    t_io = lax.broadcasted_iota(jnp.int32, (CHUNK, CHUNK), 0)
    s_io = lax.broadcasted_iota(jnp.int32, (CHUNK, CHUNK), 1)
    diff = t_io ^ s_io
    pair_level = jnp.zeros((CHUNK, CHUNK), jnp.int32)
    for m in range(LOG2_CHUNK):
        pair_level = jnp.where((diff >> m) == 1, m + 1, pair_level)
    pair_level = jnp.where(s_io > t_io, -1, pair_level)

    def scaled(vals, factors):
        return jnp.concatenate([a * b for a, b in zip(vals, factors)], axis=0).astype(BF16)

    def scores(ci, n):
        rows = slice(ci * CHUNK, (ci + 1) * CHUNK)
        cs = slice(n * HG_DK, (n + 1) * HG_DK)
        qv = _split_rows(q_ref[rows, cs])
        gv = _split_rows(gate_ref[rows, cs])
        kv = [1.0 - g for g in gv]
        levels, chunk_decay = _decay_levels(gv)
        k0 = jnp.concatenate(kv, axis=0).astype(BF16)
        attn = _dot_nt(jnp.concatenate(qv, axis=0).astype(BF16), k0)
        attn = jnp.where(pair_level == 0, attn, 0.0)
        for m in range(LOG2_CHUNK):
            e_m, f_m = levels[m]
            k_m = k0 if m == 0 else scaled(kv, f_m)
            attn = jnp.where(pair_level == m + 1, _dot_nt(scaled(qv, e_m), k_m), attn)
        e_c, f_c = levels[LOG2_CHUNK]
        return attn.astype(BF16), scaled(qv, e_c), scaled(kv, f_c), chunk_decay[0:1, :]

    def recur(ci, n, attn, q_c, k_c, decay):
        rows = slice(ci * CHUNK, (ci + 1) * CHUNK)
        cs = slice(n * HG_DK, (n + 1) * HG_DK)
        vch = v_ref[rows, cs]
        st = state_ref[n]
        o = _dot(attn, vch) + _dot_nt(q_c, st.astype(BF16))
        state_ref[n] = st * decay + _dot_tn(vch, k_c)
        on = _rmsnorm(o, gn_ref[:, cs])
        y_ref[rows, cs] = (on * sg_ref[rows, cs]).astype(BF16)

    pending = []
    for ci in range(tm // CHUNK):
        for n in range(HG_HEADS):
            pending.append((ci, n, scores(ci, n)))
            if len(pending) > SCORE_LOOKAHEAD:
                pci, pn, vals = pending.pop(0)
                recur(pci, pn, *vals)
    for pci, pn, vals in pending:
        recur(pci, pn, *vals)
    o_ref[0] = x + _dot(y_ref[...], wout_ref[...])


def _hgrn_mixer(x, g, w_in, lb_raw, g_norm, w_out, *, layer, mixer, tm=256):
    b, s, d = x.shape
    n_layers = lb_raw.shape[0]
    return pl.pallas_call(
        functools.partial(_hgrn_kernel, layer=layer),
        grid=(b, s // tm),
        in_specs=[pl.BlockSpec((1, tm, d), lambda i, j: (i, j, 0)),
                  _layer(g, layer),
                  _layer(w_in, mixer),
                  _resident((n_layers, d)),
                  _layer(g_norm, mixer),
                  _layer(w_out, mixer)],
        out_specs=pl.BlockSpec((1, tm, d), lambda i, j: (i, j, 0)),
        out_shape=jax.ShapeDtypeStruct((b, s, d), F32),
        scratch_shapes=[pltpu.VMEM((HG_HEADS, HG_DK, HG_DK), F32),
                        pltpu.VMEM((tm, d), F32),
                        pltpu.VMEM((tm, d), F32),
                        pltpu.VMEM((tm, d), BF16),
                        pltpu.VMEM((tm, d), F32),
                        pltpu.VMEM((tm, d), BF16)],
        compiler_params=_params(2),
        name="hgrn_mixer",
    )(x, g, w_in, lb_raw, g_norm, w_out)


def kernel(x, mem, norm_mix, conv_w_in, conv_w, conv_w_out, hgrn_w_in, hgrn_w_out, hgrn_norm, hgrn_lb,
           norm_xattn, norm_mem, xattn_w_q, xattn_w_kv, xattn_w_o, norm_ffn, ffn_w_in, ffn_w_out, final_norm):
    b, s, d = x.shape
    depth = norm_mix.shape[0]
    n_mixers = 2
    rows = lambda g: g.reshape(g.shape[0], 1, d)
    norm_mix, norm_xattn, norm_mem, norm_ffn, hgrn_norm = map(rows, (norm_mix, norm_xattn, norm_mem, norm_ffn,
                                                                    hgrn_norm))
    final_norm = final_norm.reshape(1, d)
    for i in range(depth):
        j = i // n_mixers
        if i % n_mixers == 0:
            x = _conv_mixer(x, norm_mix, conv_w_in, conv_w, conv_w_out, layer=i, mixer=j)
        else:
            x = _hgrn_mixer(x, norm_mix, hgrn_w_in, hgrn_lb, hgrn_norm, hgrn_w_out, layer=i, mixer=j)
        k, v = _kv_proj(mem, norm_mem, xattn_w_kv, layer=i)
        x = _xattn(x, norm_xattn, xattn_w_q, k, v, xattn_w_o, layer=i)
        x = _ffn(x.reshape(b * s, d), norm_ffn, ffn_w_in, ffn_w_out, final_norm, layer=i,
                 final_norm=(i == depth - 1)).reshape(b, s, d)
    return x
```

```python
import functools

import jax
import jax.numpy as jnp
from jax import lax
from jax.experimental import pallas as pl
from jax.experimental.pallas import tpu as pltpu

EPS = 1e-6
CHUNK = 64
LOG2_CHUNK = 6
HG_HEADS = 8
HG_DK = 128
X_HEADS = 4
X_SUBTILES = 2
CONV_W = 3
SUBLANES = 8
VMEM_LIMIT_BYTES = 56 * 1024 * 1024

F32 = jnp.float32
BF16 = jnp.bfloat16


def _rmsnorm(x, g):
    ms = jnp.mean(x * x, axis=-1, keepdims=True)
    return x * lax.rsqrt(ms + EPS) * g


def _sigmoid(x):
    return 1.0 / (1.0 + jnp.exp(-x))


def _dot(a, b):
    return jnp.dot(a.astype(BF16), b.astype(BF16), preferred_element_type=F32)


def _dot_nt(a, b):
    return lax.dot_general(a.astype(BF16), b.astype(BF16), (((1,), (1,)), ((), ())), preferred_element_type=F32)


def _dot_tn(a, b):
    return lax.dot_general(a.astype(BF16), b.astype(BF16), (((0,), (0,)), ((), ())), preferred_element_type=F32)


def _params(n_axes):
    return pltpu.CompilerParams(dimension_semantics=("arbitrary",) * n_axes,
                                vmem_limit_bytes=VMEM_LIMIT_BYTES)


def _resident(shape):
    return pl.BlockSpec(shape, lambda *_: (0,) * len(shape), pipeline_mode=pl.Buffered(1))


def _layer(stacked, layer):
    shape = stacked.shape[1:]
    return pl.BlockSpec((None,) + shape, lambda *_: (layer,) + (0,) * len(shape), pipeline_mode=pl.Buffered(1))


def _mem_kernel(mem_ref, g_ref, wkv_ref, wq_ref, wo_ref, sw_ref, vw_ref):
    m, d = mem_ref.shape[1], mem_ref.shape[2]
    hd = d // X_HEADS
    h = _rmsnorm(mem_ref[0], g_ref[...])
    kv = _dot(h, wkv_ref[...])
    k, v = kv[:, :d].astype(BF16), kv[:, d:].astype(BF16)
    for n in range(X_HEADS):
        cs = slice(n * hd, (n + 1) * hd)
        sw_ref[0, :, n * m:(n + 1) * m] = (_dot_nt(wq_ref[:, cs], k[:, cs]) * (hd ** -0.5)).astype(BF16)
        vw_ref[0, n * m:(n + 1) * m, :] = _dot(v[:, cs], wo_ref[cs, :]).astype(BF16)


def _mem_proj(mem, g, w_kv, w_q, w_o, *, layer):
    b, m, d = mem.shape
    hm = X_HEADS * m
    return pl.pallas_call(
        _mem_kernel,
        grid=(b,),
        in_specs=[pl.BlockSpec((1, m, d), lambda i: (i, 0, 0)),
                  _layer(g, layer),
                  _layer(w_kv, layer),
                  _layer(w_q, layer),
                  _layer(w_o, layer)],
        out_specs=[pl.BlockSpec((1, d, hm), lambda i: (i, 0, 0)),
                   pl.BlockSpec((1, hm, d), lambda i: (i, 0, 0))],
        out_shape=[jax.ShapeDtypeStruct((b, d, hm), BF16), jax.ShapeDtypeStruct((b, hm, d), BF16)],
        compiler_params=_params(1),
        name="mem_proj",
    )(mem, g, w_kv, w_q, w_o)


def _conv_kernel(x_ref, g_ref, win_ref, cw_ref, wout_ref, o_ref, carry_ref, *, nc):
    tm, d = x_ref.shape[1], x_ref.shape[2]

    @pl.when(pl.program_id(1) == 0)
    def _():
        carry_ref[...] = jnp.zeros_like(carry_ref)

    x = x_ref[0]
    h = _rmsnorm(x, g_ref[...]).astype(BF16)

    def project(c):
        return [_dot(h, win_ref[:, part * d + c * nc:part * d + (c + 1) * nc]) for part in range(3)]

    def gated_conv(c, gb, gc, u):
        cs = slice(c * nc, (c + 1) * nc)
        u2 = gc * u
        ext = jnp.concatenate([carry_ref[:, cs], u2], axis=0)
        s1 = pltpu.roll(ext, 1, 0)[SUBLANES:]
        s2 = pltpu.roll(ext, 2, 0)[SUBLANES:]
        z = cw_ref[0:1, cs] * s2 + cw_ref[1:2, cs] * s1 + cw_ref[2:3, cs] * u2
        carry_ref[:, cs] = u2[tm - SUBLANES:, :]
        return (gb * z).astype(BF16)

    n_chunks = d // nc
    acc = x
    proj = project(0)
    for c in range(n_chunks):
        nxt = project(c + 1) if c + 1 < n_chunks else None
        acc = acc + _dot(gated_conv(c, *proj), wout_ref[c * nc:(c + 1) * nc, :])
        proj = nxt
    o_ref[0] = acc


def _conv_mixer(x, g, w_in, cw, w_out, *, layer, mixer, tm=512, nc=256):
    b, s, d = x.shape
    return pl.pallas_call(
        functools.partial(_conv_kernel, nc=nc),
        grid=(b, s // tm),
        in_specs=[pl.BlockSpec((1, tm, d), lambda i, j: (i, j, 0)),
                  _layer(g, layer),
                  _layer(w_in, mixer),
                  _layer(cw, mixer),
                  _layer(w_out, mixer)],
        out_specs=pl.BlockSpec((1, tm, d), lambda i, j: (i, j, 0)),
        out_shape=jax.ShapeDtypeStruct((b, s, d), F32),
        scratch_shapes=[pltpu.VMEM((SUBLANES, d), F32)],
        compiler_params=_params(2),
        name="conv_mixer",
    )(x, g, w_in, cw, w_out)


def _xattn_kernel(x_ref, g_ref, sw_ref, vw_ref, o_ref):
    tm = x_ref.shape[1]
    m = sw_ref.shape[2] // X_HEADS
    ts = tm // X_SUBTILES
    subtiles = [slice(r * ts, (r + 1) * ts) for r in range(X_SUBTILES)]
    xs = [x_ref[0, rs, :] for rs in subtiles]
    scores = [_dot(_rmsnorm(x, g_ref[...]), sw_ref[0]) for x in xs]
    for rs, x, sc_r in zip(subtiles, xs, scores):
        probs = []
        for n in range(X_HEADS):
            sc = sc_r[:, n * m:(n + 1) * m]
            e = jnp.exp(sc - jnp.max(sc, axis=-1, keepdims=True))
            probs.append((e / jnp.sum(e, axis=-1, keepdims=True)).astype(BF16))
        o_ref[0, rs, :] = x + _dot(jnp.concatenate(probs, axis=-1), vw_ref[0])


def _xattn(x, g, sw, vw, *, layer, tm=512):
    b, s, d = x.shape
    hm = sw.shape[2]
    return pl.pallas_call(
        _xattn_kernel,
        grid=(b, s // tm),
        in_specs=[pl.BlockSpec((1, tm, d), lambda i, j: (i, j, 0)),
                  _layer(g, layer),
                  pl.BlockSpec((1, d, hm), lambda i, j: (i, 0, 0)),
                  pl.BlockSpec((1, hm, d), lambda i, j: (i, 0, 0))],
        out_specs=pl.BlockSpec((1, tm, d), lambda i, j: (i, j, 0)),
        out_shape=jax.ShapeDtypeStruct((b, s, d), F32),
        compiler_params=_params(2),
        name="xattn",
    )(x, g, sw, vw)


def _ffn_kernel(x_ref, g_ref, win_ref, wout_ref, gf_ref, o_ref, *, nc, final_norm):
    d_ff = wout_ref.shape[0]
    x = x_ref[...]
    h = _rmsnorm(x, g_ref[...]).astype(BF16)

    def up(c):
        return (_dot(h, win_ref[:, c * nc:(c + 1) * nc]),
                _dot(h, win_ref[:, d_ff + c * nc:d_ff + (c + 1) * nc]))

    n_chunks = d_ff // nc
    acc = x
    ab = up(0)
    for c in range(n_chunks):
        nxt = up(c + 1) if c + 1 < n_chunks else None
        a, b = ab
        act = (a * _sigmoid(a)) * b
        acc = acc + _dot(act, wout_ref[c * nc:(c + 1) * nc, :])
        ab = nxt
    if final_norm:
        acc = _rmsnorm(acc, gf_ref[...])
    o_ref[...] = acc


def _ffn(x2d, g, w_in, w_out, g_final, *, layer, final_norm, tm=512, nc=256):
    t, d = x2d.shape
    return pl.pallas_call(
        functools.partial(_ffn_kernel, nc=nc, final_norm=final_norm),
        grid=(t // tm,),
        in_specs=[pl.BlockSpec((tm, d), lambda i: (i, 0)),
                  _layer(g, layer),
                  _layer(w_in, layer),
                  _layer(w_out, layer),
                  _resident((1, d))],
        out_specs=pl.BlockSpec((tm, d), lambda i: (i, 0)),
        out_shape=jax.ShapeDtypeStruct((t, d), F32),
        compiler_params=_params(1),
        name="ffn",
    )(x2d, g, w_in, w_out, g_final)


SCORE_LOOKAHEAD = 2


def _split_rows(a):
    return [a[SUBLANES * i:SUBLANES * (i + 1)] for i in range(a.shape[0] // SUBLANES)]


def _decay_levels(gv):
    nv = len(gv)
    sub = lax.broadcasted_iota(jnp.int32, gv[0].shape, 0)
    e, f, t = list(gv), [jnp.ones_like(gv[0])] * nv, list(gv)
    levels = []
    for m in range(LOG2_CHUNK):
        levels.append((list(e), list(f)))
        half = 1 << m
        if half < SUBLANES:
            second = ((sub >> m) & 1) == 1
            for i in range(nv):
                t_prev = pltpu.roll(t[i], half, 0)
                t_next = pltpu.roll(t[i], SUBLANES - half, 0)
                e[i] = jnp.where(second, e[i] * t_prev, e[i])
                f[i] = jnp.where(second, f[i], f[i] * t_next)
                t[i] = t[i] * jnp.where(second, t_prev, t_next)
        else:
            hb = half // SUBLANES
            for blk in range(0, nv, 2 * hb):
                t_first, t_second = t[blk], t[blk + hb]
                both = t_first * t_second
                for i in range(blk, blk + hb):
                    f[i] = f[i] * t_second
                    e[i + hb] = e[i + hb] * t_first
                    t[i] = both
                    t[i + hb] = both
    levels.append((list(e), list(f)))
    return levels, t[0]


def _hgrn_kernel(x_ref, g_ref, win_ref, lb_ref, gn_ref, wout_ref, o_ref,
                 state_ref, q_ref, gate_ref, v_ref, sg_ref, y_ref, *, layer):
    tm, d = x_ref.shape[1], x_ref.shape[2]

    @pl.when(pl.program_id(1) == 0)
    def _():
        state_ref[...] = jnp.zeros_like(state_ref)

    lbr = lb_ref[...]
    e = jnp.exp(lbr - jnp.max(lbr, axis=0, keepdims=True))
    sm = e / jnp.sum(e, axis=0, keepdims=True)
    cum = sm[0:1, :]
    for i in range(1, layer + 1):
        cum = cum + sm[i:i + 1, :]
    lb = cum - sm[0:1, :]

    x = x_ref[0]
    h = _rmsnorm(x, g_ref[...]).astype(BF16)
    qf = _dot(h, win_ref[:, 0:d])
    q_ref[...] = qf * _sigmoid(qf)
    gate_ref[...] = lb + (1.0 - lb) * _sigmoid(_dot(h, win_ref[:, d:2 * d]))
    v_ref[...] = _dot(h, win_ref[:, 2 * d:3 * d]).astype(BF16)
    og = _dot(h, win_ref[:, 3 * d:4 * d])
    sg_ref[...] = og * _sigmoid(og)

    t_io = lax.broadcasted_iota(jnp.int32, (CHUNK, CHUNK), 0)
    s_io = lax.broadcasted_iota(jnp.int32, (CHUNK, CHUNK), 1)
    diff = t_io ^ s_io
    pair_level = jnp.zeros((CHUNK, CHUNK), jnp.int32)
    for m in range(LOG2_CHUNK):
        pair_level = jnp.where((diff >> m) == 1, m + 1, pair_level)
    pair_level = jnp.where(s_io > t_io, -1, pair_level)

    def scaled(vals, factors):
        return jnp.concatenate([a * b for a, b in zip(vals, factors)], axis=0).astype(BF16)

    def scores(ci, n):
        rows = slice(ci * CHUNK, (ci + 1) * CHUNK)
        cs = slice(n * HG_DK, (n + 1) * HG_DK)
        qv = _split_rows(q_ref[rows, cs])
        gv = _split_rows(gate_ref[rows, cs])
        kv = [1.0 - g for g in gv]
        levels, chunk_decay = _decay_levels(gv)
        diag = jnp.concatenate([jnp.sum(q * k, axis=-1, keepdims=True) for q, k in zip(qv, kv)], axis=0)
        pair = jnp.concatenate([jnp.sum(q * g * pltpu.roll(k, 1, 0), axis=-1, keepdims=True)
                                for q, g, k in zip(qv, gv, kv)], axis=0)
        attn = jnp.where(pair_level == 0, diag, jnp.where(pair_level == 1, pair, 0.0))
        for m in range(1, LOG2_CHUNK):
            e_m, f_m = levels[m]
            attn = jnp.where(pair_level == m + 1, _dot_nt(scaled(qv, e_m), scaled(kv, f_m)), attn)
        e_c, f_c = levels[LOG2_CHUNK]
        decay_rows = jnp.transpose(jnp.broadcast_to(chunk_decay[0:1, :], (HG_DK, HG_DK)))
        return attn.astype(BF16), scaled(qv, e_c), scaled(kv, f_c), decay_rows

    def recur(ci, n, attn, q_c, k_c, decay_rows):
        rows = slice(ci * CHUNK, (ci + 1) * CHUNK)
        cs = slice(n * HG_DK, (n + 1) * HG_DK)
        vch = v_ref[rows, cs]
        st = state_ref[n]
        o = _dot(jnp.concatenate([q_c, attn], axis=1), jnp.concatenate([st.astype(BF16), vch], axis=0))
        state_ref[n] = st * decay_rows + _dot_tn(k_c, vch)
        on = _rmsnorm(o, gn_ref[:, cs])
        y_ref[rows, cs] = (on * sg_ref[rows, cs]).astype(BF16)

    pending = []
    for ci in range(tm // CHUNK):
        for n in range(HG_HEADS):
            pending.append((ci, n, scores(ci, n)))
            if len(pending) > SCORE_LOOKAHEAD:
                pci, pn, vals = pending.pop(0)
                recur(pci, pn, *vals)
    for pci, pn, vals in pending:
        recur(pci, pn, *vals)
    o_ref[0] = x + _dot(y_ref[...], wout_ref[...])


def _hgrn_mixer(x, g, w_in, lb_raw, g_norm, w_out, *, layer, mixer, tm=256):
    b, s, d = x.shape
    n_layers = lb_raw.shape[0]
    return pl.pallas_call(
        functools.partial(_hgrn_kernel, layer=layer),
        grid=(b, s // tm),
        in_specs=[pl.BlockSpec((1, tm, d), lambda i, j: (i, j, 0)),
                  _layer(g, layer),
                  _layer(w_in, mixer),
                  _resident((n_layers, d)),
                  _layer(g_norm, mixer),
                  _layer(w_out, mixer)],
        out_specs=pl.BlockSpec((1, tm, d), lambda i, j: (i, j, 0)),
        out_shape=jax.ShapeDtypeStruct((b, s, d), F32),
        scratch_shapes=[pltpu.VMEM((HG_HEADS, HG_DK, HG_DK), F32),
                        pltpu.VMEM((tm, d), F32),
                        pltpu.VMEM((tm, d), F32),
                        pltpu.VMEM((tm, d), BF16),
                        pltpu.VMEM((tm, d), F32),
                        pltpu.VMEM((tm, d), BF16)],
        compiler_params=_params(2),
        name="hgrn_mixer",
    )(x, g, w_in, lb_raw, g_norm, w_out)


def kernel(x, mem, norm_mix, conv_w_in, conv_w, conv_w_out, hgrn_w_in, hgrn_w_out, hgrn_norm, hgrn_lb,
           norm_xattn, norm_mem, xattn_w_q, xattn_w_kv, xattn_w_o, norm_ffn, ffn_w_in, ffn_w_out, final_norm):
    b, s, d = x.shape
    depth = norm_mix.shape[0]
    n_mixers = 2
    rows = lambda g: g.reshape(g.shape[0], 1, d)
    norm_mix, norm_xattn, norm_mem, norm_ffn, hgrn_norm = map(rows, (norm_mix, norm_xattn, norm_mem, norm_ffn,
                                                                    hgrn_norm))
    final_norm = final_norm.reshape(1, d)
    for i in range(depth):
        j = i // n_mixers
        if i % n_mixers == 0:
            x = _conv_mixer(x, norm_mix, conv_w_in, conv_w, conv_w_out, layer=i, mixer=j)
        else:
            x = _hgrn_mixer(x, norm_mix, hgrn_w_in, hgrn_lb, hgrn_norm, hgrn_w_out, layer=i, mixer=j)
        sw, vw = _mem_proj(mem, norm_mem, xattn_w_kv, xattn_w_q, xattn_w_o, layer=i)
        x = _xattn(x, norm_xattn, sw, vw, layer=i)
        x = _ffn(x.reshape(b * s, d), norm_ffn, ffn_w_in, ffn_w_out, final_norm, layer=i,
                 final_norm=(i == depth - 1)).reshape(b, s, d)
    return x
```

```python
import functools

import jax
import jax.numpy as jnp
from jax import lax
from jax.experimental import pallas as pl
from jax.experimental.pallas import tpu as pltpu

EPS = 1e-6
CHUNK = 64
LOG2_CHUNK = 6
HG_HEADS = 8
HG_DK = 128
X_HEADS = 4
X_SUBTILES = 2
CONV_W = 3
SUBLANES = 8
VMEM_LIMIT_BYTES = 56 * 1024 * 1024

F32 = jnp.float32
BF16 = jnp.bfloat16


def _rmsnorm(x, g):
    ms = jnp.mean(x * x, axis=-1, keepdims=True)
    return x * lax.rsqrt(ms + EPS) * g


def _sigmoid(x):
    return 1.0 / (1.0 + jnp.exp(-x))


def _dot(a, b):
    return jnp.dot(a.astype(BF16), b.astype(BF16), preferred_element_type=F32)


def _dot_nt(a, b):
    return lax.dot_general(a.astype(BF16), b.astype(BF16), (((1,), (1,)), ((), ())), preferred_element_type=F32)


def _dot_tn(a, b):
    return lax.dot_general(a.astype(BF16), b.astype(BF16), (((0,), (0,)), ((), ())), preferred_element_type=F32)


def _params(n_axes):
    return pltpu.CompilerParams(dimension_semantics=("arbitrary",) * n_axes,
                                vmem_limit_bytes=VMEM_LIMIT_BYTES)


def _resident(shape):
    return pl.BlockSpec(shape, lambda *_: (0,) * len(shape), pipeline_mode=pl.Buffered(1))


def _layer(stacked, layer):
    shape = stacked.shape[1:]
    return pl.BlockSpec((None,) + shape, lambda *_: (layer,) + (0,) * len(shape), pipeline_mode=pl.Buffered(1))


def _mem_kernel(mem_ref, g_ref, wkv_ref, wq_ref, wo_ref, sw_ref, vw_ref):
    m, d = mem_ref.shape[1], mem_ref.shape[2]
    hd = d // X_HEADS
    h = _rmsnorm(mem_ref[0], g_ref[...])
    kv = _dot(h, wkv_ref[...])
    k, v = kv[:, :d].astype(BF16), kv[:, d:].astype(BF16)
    for n in range(X_HEADS):
        cs = slice(n * hd, (n + 1) * hd)
        sw_ref[0, :, n * m:(n + 1) * m] = (_dot_nt(wq_ref[:, cs], k[:, cs]) * (hd ** -0.5)).astype(BF16)
        vw_ref[0, n * m:(n + 1) * m, :] = _dot(v[:, cs], wo_ref[cs, :]).astype(BF16)


def _mem_proj(mem, g, w_kv, w_q, w_o, *, layer):
    b, m, d = mem.shape
    hm = X_HEADS * m
    return pl.pallas_call(
        _mem_kernel,
        grid=(b,),
        in_specs=[pl.BlockSpec((1, m, d), lambda i: (i, 0, 0)),
                  _layer(g, layer),
                  _layer(w_kv, layer),
                  _layer(w_q, layer),
                  _layer(w_o, layer)],
        out_specs=[pl.BlockSpec((1, d, hm), lambda i: (i, 0, 0)),
                   pl.BlockSpec((1, hm, d), lambda i: (i, 0, 0))],
        out_shape=[jax.ShapeDtypeStruct((b, d, hm), BF16), jax.ShapeDtypeStruct((b, hm, d), BF16)],
        compiler_params=_params(1),
        name="mem_proj",
    )(mem, g, w_kv, w_q, w_o)


def _conv_kernel(x_ref, g_ref, win_ref, cw_ref, wout_ref, o_ref, carry_ref, *, nc):
    tm, d = x_ref.shape[1], x_ref.shape[2]

    @pl.when(pl.program_id(1) == 0)
    def _():
        carry_ref[...] = jnp.zeros_like(carry_ref)

    x = x_ref[0]
    h = _rmsnorm(x, g_ref[...]).astype(BF16)

    def project(c):
        return [_dot(h, win_ref[:, part * d + c * nc:part * d + (c + 1) * nc]) for part in range(3)]

    def gated_conv(c, gb, gc, u):
        cs = slice(c * nc, (c + 1) * nc)
        u2 = gc * u
        ext = jnp.concatenate([carry_ref[:, cs], u2], axis=0)
        s1 = pltpu.roll(ext, 1, 0)[SUBLANES:]
        s2 = pltpu.roll(ext, 2, 0)[SUBLANES:]
        z = cw_ref[0:1, cs] * s2 + cw_ref[1:2, cs] * s1 + cw_ref[2:3, cs] * u2
        carry_ref[:, cs] = u2[tm - SUBLANES:, :]
        return (gb * z).astype(BF16)

    n_chunks = d // nc
    acc = x
    proj = project(0)
    for c in range(n_chunks):
        nxt = project(c + 1) if c + 1 < n_chunks else None
        acc = acc + _dot(gated_conv(c, *proj), wout_ref[c * nc:(c + 1) * nc, :])
        proj = nxt
    o_ref[0] = acc


def _conv_mixer(x, g, w_in, cw, w_out, *, layer, mixer, tm=1024, nc=256):
    b, s, d = x.shape
    return pl.pallas_call(
        functools.partial(_conv_kernel, nc=nc),
        grid=(b, s // tm),
        in_specs=[pl.BlockSpec((1, tm, d), lambda i, j: (i, j, 0)),
                  _layer(g, layer),
                  _layer(w_in, mixer),
                  _layer(cw, mixer),
                  _layer(w_out, mixer)],
        out_specs=pl.BlockSpec((1, tm, d), lambda i, j: (i, j, 0)),
        out_shape=jax.ShapeDtypeStruct((b, s, d), F32),
        scratch_shapes=[pltpu.VMEM((SUBLANES, d), F32)],
        compiler_params=_params(2),
        name="conv_mixer",
    )(x, g, w_in, cw, w_out)


def _xattn_kernel(x_ref, g_ref, sw_ref, vw_ref, o_ref):
    tm = x_ref.shape[1]
    m = sw_ref.shape[2] // X_HEADS
    ts = tm // X_SUBTILES
    subtiles = [slice(r * ts, (r + 1) * ts) for r in range(X_SUBTILES)]
    xs = [x_ref[0, rs, :] for rs in subtiles]
    scores = [_dot(_rmsnorm(x, g_ref[...]), sw_ref[0]) for x in xs]
    for rs, x, sc_r in zip(subtiles, xs, scores):
        probs = []
        for n in range(X_HEADS):
            sc = sc_r[:, n * m:(n + 1) * m]
            e = jnp.exp(sc - jnp.max(sc, axis=-1, keepdims=True))
            probs.append((e / jnp.sum(e, axis=-1, keepdims=True)).astype(BF16))
        o_ref[0, rs, :] = x + _dot(jnp.concatenate(probs, axis=-1), vw_ref[0])


def _xattn(x, g, sw, vw, *, layer, tm=1024):
    b, s, d = x.shape
    hm = sw.shape[2]
    return pl.pallas_call(
        _xattn_kernel,
        grid=(b, s // tm),
        in_specs=[pl.BlockSpec((1, tm, d), lambda i, j: (i, j, 0)),
                  _layer(g, layer),
                  pl.BlockSpec((1, d, hm), lambda i, j: (i, 0, 0)),
                  pl.BlockSpec((1, hm, d), lambda i, j: (i, 0, 0))],
        out_specs=pl.BlockSpec((1, tm, d), lambda i, j: (i, j, 0)),
        out_shape=jax.ShapeDtypeStruct((b, s, d), F32),
        compiler_params=_params(2),
        name="xattn",
    )(x, g, sw, vw)


def _ffn_kernel(x_ref, g_ref, win_ref, wout_ref, gf_ref, o_ref, *, nc, final_norm):
    d_ff = wout_ref.shape[0]
    x = x_ref[...]
    h = _rmsnorm(x, g_ref[...]).astype(BF16)

    def up(c):
        return (_dot(h, win_ref[:, c * nc:(c + 1) * nc]),
                _dot(h, win_ref[:, d_ff + c * nc:d_ff + (c + 1) * nc]))

    n_chunks = d_ff // nc
    acc = x
    ab = up(0)
    for c in range(n_chunks):
        nxt = up(c + 1) if c + 1 < n_chunks else None
        a, b = ab
        act = (a * _sigmoid(a)) * b
        acc = acc + _dot(act, wout_ref[c * nc:(c + 1) * nc, :])
        ab = nxt
    if final_norm:
        acc = _rmsnorm(acc, gf_ref[...])
    o_ref[...] = acc


def _ffn(x2d, g, w_in, w_out, g_final, *, layer, final_norm, tm=512, nc=256):
    t, d = x2d.shape
    return pl.pallas_call(
        functools.partial(_ffn_kernel, nc=nc, final_norm=final_norm),
        grid=(t // tm,),
        in_specs=[pl.BlockSpec((tm, d), lambda i: (i, 0)),
                  _layer(g, layer),
                  _layer(w_in, layer),
                  _layer(w_out, layer),
                  _resident((1, d))],
        out_specs=pl.BlockSpec((tm, d), lambda i: (i, 0)),
        out_shape=jax.ShapeDtypeStruct((t, d), F32),
        compiler_params=_params(1),
        name="ffn",
    )(x2d, g, w_in, w_out, g_final)


SCORE_LOOKAHEAD = 2


def _split_rows(a):
    return [a[SUBLANES * i:SUBLANES * (i + 1)] for i in range(a.shape[0] // SUBLANES)]


def _decay_levels(gv):
    nv = len(gv)
    sub = lax.broadcasted_iota(jnp.int32, gv[0].shape, 0)
    e, f, t = list(gv), [jnp.ones_like(gv[0])] * nv, list(gv)
    levels = []
    for m in range(LOG2_CHUNK):
        levels.append((list(e), list(f)))
        half = 1 << m
        if half < SUBLANES:
            second = ((sub >> m) & 1) == 1
            for i in range(nv):
                t_prev = pltpu.roll(t[i], half, 0)
                t_next = pltpu.roll(t[i], SUBLANES - half, 0)
                e[i] = jnp.where(second, e[i] * t_prev, e[i])
                f[i] = jnp.where(second, f[i], f[i] * t_next)
                t[i] = t[i] * jnp.where(second, t_prev, t_next)
        else:
            hb = half // SUBLANES
            for blk in range(0, nv, 2 * hb):
                t_first, t_second = t[blk], t[blk + hb]
                both = t_first * t_second
                for i in range(blk, blk + hb):
                    f[i] = f[i] * t_second
                    e[i + hb] = e[i + hb] * t_first
                    t[i] = both
                    t[i + hb] = both
    levels.append((list(e), list(f)))
    return levels, t[0]


def _hgrn_kernel(x_ref, g_ref, win_ref, lb_ref, gn_ref, wout_ref, o_ref,
                 state_ref, q_ref, gate_ref, v_ref, sg_ref, y_ref, *, layer):
    tm, d = x_ref.shape[1], x_ref.shape[2]

    @pl.when(pl.program_id(1) == 0)
    def _():
        state_ref[...] = jnp.zeros_like(state_ref)

    lbr = lb_ref[...]
    e = jnp.exp(lbr - jnp.max(lbr, axis=0, keepdims=True))
    sm = e / jnp.sum(e, axis=0, keepdims=True)
    cum = sm[0:1, :]
    for i in range(1, layer + 1):
        cum = cum + sm[i:i + 1, :]
    lb = cum - sm[0:1, :]

    x = x_ref[0]
    h = _rmsnorm(x, g_ref[...]).astype(BF16)
    qf = _dot(h, win_ref[:, 0:d])
    q_ref[...] = qf * _sigmoid(qf)
    gate_ref[...] = lb + (1.0 - lb) * _sigmoid(_dot(h, win_ref[:, d:2 * d]))
    v_ref[...] = _dot(h, win_ref[:, 2 * d:3 * d]).astype(BF16)
    og = _dot(h, win_ref[:, 3 * d:4 * d])
    sg_ref[...] = og * _sigmoid(og)

    t_io = lax.broadcasted_iota(jnp.int32, (CHUNK, CHUNK), 0)
    s_io = lax.broadcasted_iota(jnp.int32, (CHUNK, CHUNK), 1)
    diff = t_io ^ s_io
    pair_level = jnp.zeros((CHUNK, CHUNK), jnp.int32)
    for m in range(LOG2_CHUNK):
        pair_level = jnp.where((diff >> m) == 1, m + 1, pair_level)
    pair_level = jnp.where(s_io > t_io, -1, pair_level)

    def scaled(vals, factors):
        return jnp.concatenate([a * b for a, b in zip(vals, factors)], axis=0).astype(BF16)

    def scores(ci, n):
        rows = slice(ci * CHUNK, (ci + 1) * CHUNK)
        cs = slice(n * HG_DK, (n + 1) * HG_DK)
        qv = _split_rows(q_ref[rows, cs])
        gv = _split_rows(gate_ref[rows, cs])
        kv = [1.0 - g for g in gv]
        levels, chunk_decay = _decay_levels(gv)
        diag = jnp.concatenate([jnp.sum(q * k, axis=-1, keepdims=True) for q, k in zip(qv, kv)], axis=0)
        pair = jnp.concatenate([jnp.sum(q * g * pltpu.roll(k, 1, 0), axis=-1, keepdims=True)
                                for q, g, k in zip(qv, gv, kv)], axis=0)
        attn = jnp.where(pair_level == 0, diag, jnp.where(pair_level == 1, pair, 0.0))
        for m in range(1, LOG2_CHUNK):
            e_m, f_m = levels[m]
            attn = jnp.where(pair_level == m + 1, _dot_nt(scaled(qv, e_m), scaled(kv, f_m)), attn)
        e_c, f_c = levels[LOG2_CHUNK]
        decay_rows = jnp.transpose(jnp.broadcast_to(chunk_decay[0:1, :], (HG_DK, HG_DK)))
        return attn.astype(BF16), scaled(qv, e_c), scaled(kv, f_c), decay_rows

    def recur(ci, n, attn, q_c, k_c, decay_rows):
        rows = slice(ci * CHUNK, (ci + 1) * CHUNK)
        cs = slice(n * HG_DK, (n + 1) * HG_DK)
        vch = v_ref[rows, cs]
        st = state_ref[n]
        o = _dot(jnp.concatenate([q_c, attn], axis=1), jnp.concatenate([st.astype(BF16), vch], axis=0))
        state_ref[n] = st * decay_rows + _dot_tn(k_c, vch)
        on = _rmsnorm(o, gn_ref[:, cs])
        y_ref[rows, cs] = (on * sg_ref[rows, cs]).astype(BF16)

    pending = []
    for ci in range(tm // CHUNK):
        for n in range(HG_HEADS):
            pending.append((ci, n, scores(ci, n)))
            if len(pending) > SCORE_LOOKAHEAD:
                pci, pn, vals = pending.pop(0)
                recur(pci, pn, *vals)
    for pci, pn, vals in pending:
        recur(pci, pn, *vals)
    o_ref[0] = x + _dot(y_ref[...], wout_ref[...])


def _hgrn_mixer(x, g, w_in, lb_raw, g_norm, w_out, *, layer, mixer, tm=512):
    b, s, d = x.shape
    n_layers = lb_raw.shape[0]
    return pl.pallas_call(
        functools.partial(_hgrn_kernel, layer=layer),
        grid=(b, s // tm),
        in_specs=[pl.BlockSpec((1, tm, d), lambda i, j: (i, j, 0)),
                  _layer(g, layer),
                  _layer(w_in, mixer),
                  _resident((n_layers, d)),
                  _layer(g_norm, mixer),
                  _layer(w_out, mixer)],
        out_specs=pl.BlockSpec((1, tm, d), lambda i, j: (i, j, 0)),
        out_shape=jax.ShapeDtypeStruct((b, s, d), F32),
        scratch_shapes=[pltpu.VMEM((HG_HEADS, HG_DK, HG_DK), F32),
                        pltpu.VMEM((tm, d), F32),
                        pltpu.VMEM((tm, d), F32),
                        pltpu.VMEM((tm, d), BF16),
                        pltpu.VMEM((tm, d), F32),
                        pltpu.VMEM((tm, d), BF16)],
        compiler_params=_params(2),
        name="hgrn_mixer",
    )(x, g, w_in, lb_raw, g_norm, w_out)


def kernel(x, mem, norm_mix, conv_w_in, conv_w, conv_w_out, hgrn_w_in, hgrn_w_out, hgrn_norm, hgrn_lb,
           norm_xattn, norm_mem, xattn_w_q, xattn_w_kv, xattn_w_o, norm_ffn, ffn_w_in, ffn_w_out, final_norm):
    b, s, d = x.shape
    depth = norm_mix.shape[0]
    n_mixers = 2
    rows = lambda g: g.reshape(g.shape[0], 1, d)
    norm_mix, norm_xattn, norm_mem, norm_ffn, hgrn_norm = map(rows, (norm_mix, norm_xattn, norm_mem, norm_ffn,
                                                                    hgrn_norm))
    final_norm = final_norm.reshape(1, d)
    for i in range(depth):
        j = i // n_mixers
        if i % n_mixers == 0:
            x = _conv_mixer(x, norm_mix, conv_w_in, conv_w, conv_w_out, layer=i, mixer=j)
        else:
            x = _hgrn_mixer(x, norm_mix, hgrn_w_in, hgrn_lb, hgrn_norm, hgrn_w_out, layer=i, mixer=j)
        sw, vw = _mem_proj(mem, norm_mem, xattn_w_kv, xattn_w_q, xattn_w_o, layer=i)
        x = _xattn(x, norm_xattn, sw, vw, layer=i)
        x = _ffn(x.reshape(b * s, d), norm_ffn, ffn_w_in, ffn_w_out, final_norm, layer=i,
                 final_norm=(i == depth - 1)).reshape(b, s, d)
    return x
```

```python
import functools

import jax
import jax.numpy as jnp
from jax import lax
from jax.experimental import pallas as pl
from jax.experimental.pallas import tpu as pltpu

EPS = 1e-6
CHUNK = 64
LOG2_CHUNK = 6
HG_HEADS = 8
HG_DK = 128
X_HEADS = 4
X_SUBTILES = 2
CONV_W = 3
SUBLANES = 8
VMEM_LIMIT_BYTES = 56 * 1024 * 1024

F32 = jnp.float32
BF16 = jnp.bfloat16


def _rmsnorm(x, g):
    ms = jnp.mean(x * x, axis=-1, keepdims=True)
    return x * lax.rsqrt(ms + EPS) * g


def _sigmoid(x):
    return 1.0 / (1.0 + jnp.exp(-x))


def _dot(a, b):
    return jnp.dot(a.astype(BF16), b.astype(BF16), preferred_element_type=F32)


def _dot_nt(a, b):
    return lax.dot_general(a.astype(BF16), b.astype(BF16), (((1,), (1,)), ((), ())), preferred_element_type=F32)


def _dot_tn(a, b):
    return lax.dot_general(a.astype(BF16), b.astype(BF16), (((0,), (0,)), ((), ())), preferred_element_type=F32)


def _params(n_axes):
    return pltpu.CompilerParams(dimension_semantics=("arbitrary",) * n_axes,
                                vmem_limit_bytes=VMEM_LIMIT_BYTES)


def _resident(shape):
    return pl.BlockSpec(shape, lambda *_: (0,) * len(shape), pipeline_mode=pl.Buffered(1))


def _layer(stacked, layer):
    shape = stacked.shape[1:]
    return pl.BlockSpec((None,) + shape, lambda *_: (layer,) + (0,) * len(shape), pipeline_mode=pl.Buffered(1))


def _mem_kernel(mem_ref, g_ref, wkv_ref, wq_ref, wo_ref, sw_ref, vw_ref):
    m, d = mem_ref.shape[1], mem_ref.shape[2]
    hd = d // X_HEADS
    h = _rmsnorm(mem_ref[0], g_ref[...])
    kv = _dot(h, wkv_ref[...])
    k, v = kv[:, :d].astype(BF16), kv[:, d:].astype(BF16)
    for n in range(X_HEADS):
        cs = slice(n * hd, (n + 1) * hd)
        sw_ref[0, :, n * m:(n + 1) * m] = (_dot_nt(wq_ref[:, cs], k[:, cs]) * (hd ** -0.5)).astype(BF16)
        vw_ref[0, n * m:(n + 1) * m, :] = _dot(v[:, cs], wo_ref[cs, :]).astype(BF16)


def _mem_proj(mem, g, w_kv, w_q, w_o, *, layer):
    b, m, d = mem.shape
    hm = X_HEADS * m
    return pl.pallas_call(
        _mem_kernel,
        grid=(b,),
        in_specs=[pl.BlockSpec((1, m, d), lambda i: (i, 0, 0)),
                  _layer(g, layer),
                  _layer(w_kv, layer),
                  _layer(w_q, layer),
                  _layer(w_o, layer)],
        out_specs=[pl.BlockSpec((1, d, hm), lambda i: (i, 0, 0)),
                   pl.BlockSpec((1, hm, d), lambda i: (i, 0, 0))],
        out_shape=[jax.ShapeDtypeStruct((b, d, hm), BF16), jax.ShapeDtypeStruct((b, hm, d), BF16)],
        compiler_params=_params(1),
        name="mem_proj",
    )(mem, g, w_kv, w_q, w_o)


def _conv_kernel(x_ref, g_ref, win_ref, cw_ref, wout_ref, o_ref, carry_ref, *, nc):
    tm, d = x_ref.shape[1], x_ref.shape[2]

    @pl.when(pl.program_id(1) == 0)
    def _():
        carry_ref[...] = jnp.zeros_like(carry_ref)

    x = x_ref[0]
    h = _rmsnorm(x, g_ref[...]).astype(BF16)

    def project(c):
        return [_dot(h, win_ref[:, part * d + c * nc:part * d + (c + 1) * nc]) for part in range(3)]

    def gated_conv(c, gb, gc, u):
        cs = slice(c * nc, (c + 1) * nc)
        u2 = gc * u
        ext = jnp.concatenate([carry_ref[:, cs], u2], axis=0)
        s1 = pltpu.roll(ext, 1, 0)[SUBLANES:]
        s2 = pltpu.roll(ext, 2, 0)[SUBLANES:]
        z = cw_ref[0:1, cs] * s2 + cw_ref[1:2, cs] * s1 + cw_ref[2:3, cs] * u2
        carry_ref[:, cs] = u2[tm - SUBLANES:, :]
        return (gb * z).astype(BF16)

    n_chunks = d // nc
    acc = x
    proj = project(0)
    for c in range(n_chunks):
        nxt = project(c + 1) if c + 1 < n_chunks else None
        acc = acc + _dot(gated_conv(c, *proj), wout_ref[c * nc:(c + 1) * nc, :])
        proj = nxt
    o_ref[0] = acc


def _conv_mixer(x, g, w_in, cw, w_out, *, layer, mixer, tm=1024, nc=256):
    b, s, d = x.shape
    return pl.pallas_call(
        functools.partial(_conv_kernel, nc=nc),
        grid=(b, s // tm),
        in_specs=[pl.BlockSpec((1, tm, d), lambda i, j: (i, j, 0)),
                  _layer(g, layer),
                  _layer(w_in, mixer),
                  _layer(cw, mixer),
                  _layer(w_out, mixer)],
        out_specs=pl.BlockSpec((1, tm, d), lambda i, j: (i, j, 0)),
        out_shape=jax.ShapeDtypeStruct((b, s, d), F32),
        scratch_shapes=[pltpu.VMEM((SUBLANES, d), F32)],
        compiler_params=_params(2),
        name="conv_mixer",
    )(x, g, w_in, cw, w_out)


def _xattn_kernel(x_ref, g_ref, sw_ref, vw_ref, o_ref):
    tm = x_ref.shape[1]
    m = sw_ref.shape[2] // X_HEADS
    ts = tm // X_SUBTILES
    subtiles = [slice(r * ts, (r + 1) * ts) for r in range(X_SUBTILES)]
    xs = [x_ref[0, rs, :] for rs in subtiles]
    scores = [_dot(_rmsnorm(x, g_ref[...]), sw_ref[0]) for x in xs]
    for rs, x, sc_r in zip(subtiles, xs, scores):
        probs = []
        for n in range(X_HEADS):
            sc = sc_r[:, n * m:(n + 1) * m]
            e = jnp.exp(sc - jnp.max(sc, axis=-1, keepdims=True))
            probs.append((e / jnp.sum(e, axis=-1, keepdims=True)).astype(BF16))
        o_ref[0, rs, :] = x + _dot(jnp.concatenate(probs, axis=-1), vw_ref[0])


def _xattn(x, g, sw, vw, *, layer, tm=1024):
    b, s, d = x.shape
    hm = sw.shape[2]
    return pl.pallas_call(
        _xattn_kernel,
        grid=(b, s // tm),
        in_specs=[pl.BlockSpec((1, tm, d), lambda i, j: (i, j, 0)),
                  _layer(g, layer),
                  pl.BlockSpec((1, d, hm), lambda i, j: (i, 0, 0)),
                  pl.BlockSpec((1, hm, d), lambda i, j: (i, 0, 0))],
        out_specs=pl.BlockSpec((1, tm, d), lambda i, j: (i, j, 0)),
        out_shape=jax.ShapeDtypeStruct((b, s, d), F32),
        compiler_params=_params(2),
        name="xattn",
    )(x, g, sw, vw)


FFN_STAGE_COPIES = 8


def _stage_bf16(w_hbm, layer, dst_ref, buf_ref, sem_ref):
    slab = buf_ref.shape[1]
    n = dst_ref.shape[0] // slab

    def copy(i):
        return pltpu.make_async_copy(w_hbm.at[layer, pl.ds(i * slab, slab), :], buf_ref.at[i % 2], sem_ref.at[i % 2])

    copy(0).start()
    for i in range(n):
        if i + 1 < n:
            copy(i + 1).start()
        copy(i).wait()
        dst_ref[i * slab:(i + 1) * slab, :] = buf_ref[i % 2].astype(BF16)


def _ffn_kernel(x_ref, g_ref, win_hbm, wout_hbm, gf_ref, o_ref,
                win_ref, wout_ref, win_buf, wout_buf, win_sem, wout_sem, *, layer, nc, final_norm):
    @pl.when(pl.program_id(0) == 0)
    def _():
        _stage_bf16(win_hbm, layer, win_ref, win_buf, win_sem)
        _stage_bf16(wout_hbm, layer, wout_ref, wout_buf, wout_sem)

    d_ff = wout_ref.shape[0]
    x = x_ref[...]
    h = _rmsnorm(x, g_ref[...]).astype(BF16)

    def up(c):
        return (_dot(h, win_ref[:, c * nc:(c + 1) * nc]),
                _dot(h, win_ref[:, d_ff + c * nc:d_ff + (c + 1) * nc]))

    n_chunks = d_ff // nc
    acc = x
    ab = up(0)
    for c in range(n_chunks):
        nxt = up(c + 1) if c + 1 < n_chunks else None
        a, b = ab
        act = (a * _sigmoid(a)) * b
        acc = acc + _dot(act, wout_ref[c * nc:(c + 1) * nc, :])
        ab = nxt
    if final_norm:
        acc = _rmsnorm(acc, gf_ref[...])
    o_ref[...] = acc


def _ffn(x2d, g, w_in, w_out, g_final, *, layer, final_norm, tm=1024, nc=256):
    t, d = x2d.shape
    d_ff = w_out.shape[1]
    return pl.pallas_call(
        functools.partial(_ffn_kernel, layer=layer, nc=nc, final_norm=final_norm),
        grid=(t // tm,),
        in_specs=[pl.BlockSpec((tm, d), lambda i: (i, 0)),
                  _layer(g, layer),
                  pl.BlockSpec(memory_space=pl.ANY),
                  pl.BlockSpec(memory_space=pl.ANY),
                  _resident((1, d))],
        out_specs=pl.BlockSpec((tm, d), lambda i: (i, 0)),
        out_shape=jax.ShapeDtypeStruct((t, d), F32),
        scratch_shapes=[pltpu.VMEM((d, 2 * d_ff), BF16),
                        pltpu.VMEM((d_ff, d), BF16),
                        pltpu.VMEM((2, d // FFN_STAGE_COPIES, 2 * d_ff), F32),
                        pltpu.VMEM((2, d_ff // FFN_STAGE_COPIES, d), F32),
                        pltpu.SemaphoreType.DMA((2,)),
                        pltpu.SemaphoreType.DMA((2,))],
        compiler_params=_params(1),
        name="ffn",
    )(x2d, g, w_in, w_out, g_final)


SCORE_LOOKAHEAD = 2


def _split_rows(a):
    return [a[SUBLANES * i:SUBLANES * (i + 1)] for i in range(a.shape[0] // SUBLANES)]


def _decay_levels(gv):
    nv = len(gv)
    sub = lax.broadcasted_iota(jnp.int32, gv[0].shape, 0)
    e, f, t = list(gv), [jnp.ones_like(gv[0])] * nv, list(gv)
    levels = []
    for m in range(LOG2_CHUNK):
        levels.append((list(e), list(f)))
        half = 1 << m
        if half < SUBLANES:
            second = ((sub >> m) & 1) == 1
            for i in range(nv):
                t_prev = pltpu.roll(t[i], half, 0)
                t_next = pltpu.roll(t[i], SUBLANES - half, 0)
                e[i] = jnp.where(second, e[i] * t_prev, e[i])
                f[i] = jnp.where(second, f[i], f[i] * t_next)
                t[i] = t[i] * jnp.where(second, t_prev, t_next)
        else:
            hb = half // SUBLANES
            for blk in range(0, nv, 2 * hb):
                t_first, t_second = t[blk], t[blk + hb]
                both = t_first * t_second
                for i in range(blk, blk + hb):
                    f[i] = f[i] * t_second
                    e[i + hb] = e[i + hb] * t_first
                    t[i] = both
                    t[i + hb] = both
    levels.append((list(e), list(f)))
    return levels, t[0]


def _hgrn_kernel(x_ref, g_ref, win_ref, lb_ref, gn_ref, wout_ref, o_ref,
                 state_ref, q_ref, gate_ref, v_ref, sg_ref, y_ref, *, layer):
    tm, d = x_ref.shape[1], x_ref.shape[2]

    @pl.when(pl.program_id(1) == 0)
    def _():
        state_ref[...] = jnp.zeros_like(state_ref)

    lbr = lb_ref[...]
    e = jnp.exp(lbr - jnp.max(lbr, axis=0, keepdims=True))
    sm = e / jnp.sum(e, axis=0, keepdims=True)
    cum = sm[0:1, :]
    for i in range(1, layer + 1):
        cum = cum + sm[i:i + 1, :]
    lb = cum - sm[0:1, :]

    x = x_ref[0]
    h = _rmsnorm(x, g_ref[...]).astype(BF16)
    qf = _dot(h, win_ref[:, 0:d])
    q_ref[...] = qf * _sigmoid(qf)
    gate_ref[...] = lb + (1.0 - lb) * _sigmoid(_dot(h, win_ref[:, d:2 * d]))
    v_ref[...] = _dot(h, win_ref[:, 2 * d:3 * d]).astype(BF16)
    og = _dot(h, win_ref[:, 3 * d:4 * d])
    sg_ref[...] = og * _sigmoid(og)

    t_io = lax.broadcasted_iota(jnp.int32, (CHUNK, CHUNK), 0)
    s_io = lax.broadcasted_iota(jnp.int32, (CHUNK, CHUNK), 1)
    diff = t_io ^ s_io
    pair_level = jnp.zeros((CHUNK, CHUNK), jnp.int32)
    for m in range(LOG2_CHUNK):
        pair_level = jnp.where((diff >> m) == 1, m + 1, pair_level)
    pair_level = jnp.where(s_io > t_io, -1, pair_level)

    def scaled(vals, factors):
        return jnp.concatenate([a * b for a, b in zip(vals, factors)], axis=0).astype(BF16)

    def scores(ci, n):
        rows = slice(ci * CHUNK, (ci + 1) * CHUNK)
        cs = slice(n * HG_DK, (n + 1) * HG_DK)
        qv = _split_rows(q_ref[rows, cs])
        gv = _split_rows(gate_ref[rows, cs])
        kv = [1.0 - g for g in gv]
        levels, chunk_decay = _decay_levels(gv)
        diag = jnp.concatenate([jnp.sum(q * k, axis=-1, keepdims=True) for q, k in zip(qv, kv)], axis=0)
        pair = jnp.concatenate([jnp.sum(q * g * pltpu.roll(k, 1, 0), axis=-1, keepdims=True)
                                for q, g, k in zip(qv, gv, kv)], axis=0)
        attn = jnp.where(pair_level == 0, diag, jnp.where(pair_level == 1, pair, 0.0))
        for m in range(1, LOG2_CHUNK):
            e_m, f_m = levels[m]
            attn = jnp.where(pair_level == m + 1, _dot_nt(scaled(qv, e_m), scaled(kv, f_m)), attn)
        e_c, f_c = levels[LOG2_CHUNK]
        decay_rows = jnp.transpose(jnp.broadcast_to(chunk_decay[0:1, :], (HG_DK, HG_DK)))
        return attn.astype(BF16), scaled(qv, e_c), scaled(kv, f_c), decay_rows

    def recur(ci, n, attn, q_c, k_c, decay_rows):
        rows = slice(ci * CHUNK, (ci + 1) * CHUNK)
        cs = slice(n * HG_DK, (n + 1) * HG_DK)
        vch = v_ref[rows, cs]
        st = state_ref[n]
        o = _dot(jnp.concatenate([q_c, attn], axis=1), jnp.concatenate([st.astype(BF16), vch], axis=0))
        state_ref[n] = st * decay_rows + _dot_tn(k_c, vch)
        on = _rmsnorm(o, gn_ref[:, cs])
        y_ref[rows, cs] = (on * sg_ref[rows, cs]).astype(BF16)

    pending = []
    for ci in range(tm // CHUNK):
        for n in range(HG_HEADS):
            pending.append((ci, n, scores(ci, n)))
            if len(pending) > SCORE_LOOKAHEAD:
                pci, pn, vals = pending.pop(0)
                recur(pci, pn, *vals)
    for pci, pn, vals in pending:
        recur(pci, pn, *vals)
    o_ref[0] = x + _dot(y_ref[...], wout_ref[...])


def _hgrn_mixer(x, g, w_in, lb_raw, g_norm, w_out, *, layer, mixer, tm=512):
    b, s, d = x.shape
    n_layers = lb_raw.shape[0]
    return pl.pallas_call(
        functools.partial(_hgrn_kernel, layer=layer),
        grid=(b, s // tm),
        in_specs=[pl.BlockSpec((1, tm, d), lambda i, j: (i, j, 0)),
                  _layer(g, layer),
                  _layer(w_in, mixer),
                  _resident((n_layers, d)),
                  _layer(g_norm, mixer),
                  _layer(w_out, mixer)],
        out_specs=pl.BlockSpec((1, tm, d), lambda i, j: (i, j, 0)),
        out_shape=jax.ShapeDtypeStruct((b, s, d), F32),
        scratch_shapes=[pltpu.VMEM((HG_HEADS, HG_DK, HG_DK), F32),
                        pltpu.VMEM((tm, d), F32),
                        pltpu.VMEM((tm, d), F32),
                        pltpu.VMEM((tm, d), BF16),
                        pltpu.VMEM((tm, d), F32),
                        pltpu.VMEM((tm, d), BF16)],
        compiler_params=_params(2),
        name="hgrn_mixer",
    )(x, g, w_in, lb_raw, g_norm, w_out)


def kernel(x, mem, norm_mix, conv_w_in, conv_w, conv_w_out, hgrn_w_in, hgrn_w_out, hgrn_norm, hgrn_lb,
           norm_xattn, norm_mem, xattn_w_q, xattn_w_kv, xattn_w_o, norm_ffn, ffn_w_in, ffn_w_out, final_norm):
    b, s, d = x.shape
    depth = norm_mix.shape[0]
    n_mixers = 2
    rows = lambda g: g.reshape(g.shape[0], 1, d)
    norm_mix, norm_xattn, norm_mem, norm_ffn, hgrn_norm = map(rows, (norm_mix, norm_xattn, norm_mem, norm_ffn,
                                                                    hgrn_norm))
    final_norm = final_norm.reshape(1, d)
    for i in range(depth):
        j = i // n_mixers
        if i % n_mixers == 0:
            x = _conv_mixer(x, norm_mix, conv_w_in, conv_w, conv_w_out, layer=i, mixer=j)
        else:
            x = _hgrn_mixer(x, norm_mix, hgrn_w_in, hgrn_lb, hgrn_norm, hgrn_w_out, layer=i, mixer=j)
        sw, vw = _mem_proj(mem, norm_mem, xattn_w_kv, xattn_w_q, xattn_w_o, layer=i)
        x = _xattn(x, norm_xattn, sw, vw, layer=i)
        x = _ffn(x.reshape(b * s, d), norm_ffn, ffn_w_in, ffn_w_out, final_norm, layer=i,
                 final_norm=(i == depth - 1)).reshape(b, s, d)
    return x
```

```python
import functools

import jax
import jax.numpy as jnp
from jax import lax
from jax.experimental import pallas as pl
from jax.experimental.pallas import tpu as pltpu

EPS = 1e-6
CHUNK = 64
LOG2_CHUNK = 6
HG_HEADS = 8
HG_DK = 128
X_HEADS = 4
X_SUBTILES = 4
CONV_W = 3
SUBLANES = 8
VMEM_LIMIT_BYTES = 56 * 1024 * 1024

F32 = jnp.float32
BF16 = jnp.bfloat16


def _rmsnorm(x, g):
    ms = jnp.mean(x * x, axis=-1, keepdims=True)
    return x * lax.rsqrt(ms + EPS) * g


def _sigmoid(x):
    return 1.0 / (1.0 + jnp.exp(-x))


def _dot(a, b):
    return jnp.dot(a.astype(BF16), b.astype(BF16), preferred_element_type=F32)


def _dot_nt(a, b):
    return lax.dot_general(a.astype(BF16), b.astype(BF16), (((1,), (1,)), ((), ())), preferred_element_type=F32)


def _dot_tn(a, b):
    return lax.dot_general(a.astype(BF16), b.astype(BF16), (((0,), (0,)), ((), ())), preferred_element_type=F32)


def _params(n_axes):
    return pltpu.CompilerParams(dimension_semantics=("arbitrary",) * n_axes,
                                vmem_limit_bytes=VMEM_LIMIT_BYTES)


def _resident(shape):
    return pl.BlockSpec(shape, lambda *_: (0,) * len(shape), pipeline_mode=pl.Buffered(1))


def _layer(stacked, layer):
    shape = stacked.shape[1:]
    return pl.BlockSpec((None,) + shape, lambda *_: (layer,) + (0,) * len(shape), pipeline_mode=pl.Buffered(1))


def _mem_kernel(mem_ref, g_ref, wkv_ref, wq_ref, wo_ref, sw_ref, vw_ref):
    m, d = mem_ref.shape[1], mem_ref.shape[2]
    hd = d // X_HEADS
    h = _rmsnorm(mem_ref[0], g_ref[...])
    kv = _dot(h, wkv_ref[...])
    k, v = kv[:, :d].astype(BF16), kv[:, d:].astype(BF16)
    for n in range(X_HEADS):
        cs = slice(n * hd, (n + 1) * hd)
        sw_ref[0, :, n * m:(n + 1) * m] = (_dot_nt(wq_ref[:, cs], k[:, cs]) * (hd ** -0.5)).astype(BF16)
        vw_ref[0, n * m:(n + 1) * m, :] = _dot(v[:, cs], wo_ref[cs, :]).astype(BF16)


def _mem_proj(mem, g, w_kv, w_q, w_o, *, layer):
    b, m, d = mem.shape
    hm = X_HEADS * m
    return pl.pallas_call(
        _mem_kernel,
        grid=(b,),
        in_specs=[pl.BlockSpec((1, m, d), lambda i: (i, 0, 0)),
                  _layer(g, layer),
                  _layer(w_kv, layer),
                  _layer(w_q, layer),
                  _layer(w_o, layer)],
        out_specs=[pl.BlockSpec((1, d, hm), lambda i: (i, 0, 0)),
                   pl.BlockSpec((1, hm, d), lambda i: (i, 0, 0))],
        out_shape=[jax.ShapeDtypeStruct((b, d, hm), BF16), jax.ShapeDtypeStruct((b, hm, d), BF16)],
        compiler_params=_params(1),
        name="mem_proj",
    )(mem, g, w_kv, w_q, w_o)


def _conv_kernel(x_ref, g_ref, win_ref, cw_ref, wout_ref, o_ref, carry_ref, *, nc):
    tm, d = x_ref.shape[1], x_ref.shape[2]

    @pl.when(pl.program_id(1) == 0)
    def _():
        carry_ref[...] = jnp.zeros_like(carry_ref)

    x = x_ref[0]
    h = _rmsnorm(x, g_ref[...]).astype(BF16)

    def project(c):
        return [_dot(h, win_ref[:, part * d + c * nc:part * d + (c + 1) * nc]) for part in range(3)]

    def gated_conv(c, gb, gc, u):
        cs = slice(c * nc, (c + 1) * nc)
        u2 = gc * u
        ext = jnp.concatenate([carry_ref[:, cs], u2], axis=0)
        s1 = pltpu.roll(ext, 1, 0)[SUBLANES:]
        s2 = pltpu.roll(ext, 2, 0)[SUBLANES:]
        z = cw_ref[0:1, cs] * s2 + cw_ref[1:2, cs] * s1 + cw_ref[2:3, cs] * u2
        carry_ref[:, cs] = u2[tm - SUBLANES:, :]
        return (gb * z).astype(BF16)

    n_chunks = d // nc
    acc = x
    proj = project(0)
    for c in range(n_chunks):
        nxt = project(c + 1) if c + 1 < n_chunks else None
        acc = acc + _dot(gated_conv(c, *proj), wout_ref[c * nc:(c + 1) * nc, :])
        proj = nxt
    o_ref[0] = acc


def _conv_mixer(x, g, w_in, cw, w_out, *, layer, mixer, tm=1024, nc=256):
    b, s, d = x.shape
    return pl.pallas_call(
        functools.partial(_conv_kernel, nc=nc),
        grid=(b, s // tm),
        in_specs=[pl.BlockSpec((1, tm, d), lambda i, j: (i, j, 0)),
                  _layer(g, layer),
                  _layer(w_in, mixer),
                  _layer(cw, mixer),
                  _layer(w_out, mixer)],
        out_specs=pl.BlockSpec((1, tm, d), lambda i, j: (i, j, 0)),
        out_shape=jax.ShapeDtypeStruct((b, s, d), F32),
        scratch_shapes=[pltpu.VMEM((SUBLANES, d), F32)],
        compiler_params=_params(2),
        name="conv_mixer",
    )(x, g, w_in, cw, w_out)


def _xattn_kernel(x_ref, g_ref, sw_ref, vw_ref, o_ref):
    tm = x_ref.shape[1]
    m = sw_ref.shape[2] // X_HEADS
    ts = tm // X_SUBTILES
    subtiles = [slice(r * ts, (r + 1) * ts) for r in range(X_SUBTILES)]
    xs = [x_ref[0, rs, :] for rs in subtiles]
    scores = [_dot(_rmsnorm(x, g_ref[...]), sw_ref[0]) for x in xs]
    for rs, x, sc_r in zip(subtiles, xs, scores):
        probs = []
        for n in range(X_HEADS):
            sc = sc_r[:, n * m:(n + 1) * m]
            e = jnp.exp(sc - jnp.max(sc, axis=-1, keepdims=True))
            probs.append((e / jnp.sum(e, axis=-1, keepdims=True)).astype(BF16))
        o_ref[0, rs, :] = x + _dot(jnp.concatenate(probs, axis=-1), vw_ref[0])


def _xattn(x, g, sw, vw, *, layer, tm=2048):
    b, s, d = x.shape
    hm = sw.shape[2]
    return pl.pallas_call(
        _xattn_kernel,
        grid=(b, s // tm),
        in_specs=[pl.BlockSpec((1, tm, d), lambda i, j: (i, j, 0)),
                  _layer(g, layer),
                  pl.BlockSpec((1, d, hm), lambda i, j: (i, 0, 0)),
                  pl.BlockSpec((1, hm, d), lambda i, j: (i, 0, 0))],
        out_specs=pl.BlockSpec((1, tm, d), lambda i, j: (i, j, 0)),
        out_shape=jax.ShapeDtypeStruct((b, s, d), F32),
        compiler_params=_params(2),
        name="xattn",
    )(x, g, sw, vw)


def _ffn_kernel(x_ref, g_ref, win_ref, wout_ref, gf_ref, o_ref, *, nc, final_norm):
    d_ff = wout_ref.shape[0]
    x = x_ref[...]
    h = _rmsnorm(x, g_ref[...]).astype(BF16)

    def up(c):
        return (_dot(h, win_ref[:, c * nc:(c + 1) * nc]),
                _dot(h, win_ref[:, d_ff + c * nc:d_ff + (c + 1) * nc]))

    n_chunks = d_ff // nc
    acc = x
    ab = up(0)
    for c in range(n_chunks):
        nxt = up(c + 1) if c + 1 < n_chunks else None
        a, b = ab
        act = (a * _sigmoid(a)) * b
        acc = acc + _dot(act, wout_ref[c * nc:(c + 1) * nc, :])
        ab = nxt
    if final_norm:
        acc = _rmsnorm(acc, gf_ref[...])
    o_ref[...] = acc


def _ffn(x2d, g, w_in, w_out, g_final, *, layer, final_norm, tm=512, nc=256):
    t, d = x2d.shape
    return pl.pallas_call(
        functools.partial(_ffn_kernel, nc=nc, final_norm=final_norm),
        grid=(t // tm,),
        in_specs=[pl.BlockSpec((tm, d), lambda i: (i, 0)),
                  _layer(g, layer),
                  _layer(w_in, layer),
                  _layer(w_out, layer),
                  _resident((1, d))],
        out_specs=pl.BlockSpec((tm, d), lambda i: (i, 0)),
        out_shape=jax.ShapeDtypeStruct((t, d), F32),
        compiler_params=_params(1),
        name="ffn",
    )(x2d, g, w_in, w_out, g_final)


SCORE_LOOKAHEAD = 2


def _split_rows(a):
    return [a[SUBLANES * i:SUBLANES * (i + 1)] for i in range(a.shape[0] // SUBLANES)]


def _decay_levels(gv):
    nv = len(gv)
    sub = lax.broadcasted_iota(jnp.int32, gv[0].shape, 0)
    e, f, t = list(gv), [jnp.ones_like(gv[0])] * nv, list(gv)
    levels = []
    for m in range(LOG2_CHUNK):
        levels.append((list(e), list(f)))
        half = 1 << m
        if half < SUBLANES:
            second = ((sub >> m) & 1) == 1
            for i in range(nv):
                t_prev = pltpu.roll(t[i], half, 0)
                t_next = pltpu.roll(t[i], SUBLANES - half, 0)
                e[i] = jnp.where(second, e[i] * t_prev, e[i])
                f[i] = jnp.where(second, f[i], f[i] * t_next)
                t[i] = t[i] * jnp.where(second, t_prev, t_next)
        else:
            hb = half // SUBLANES
            for blk in range(0, nv, 2 * hb):
                t_first, t_second = t[blk], t[blk + hb]
                both = t_first * t_second
                for i in range(blk, blk + hb):
                    f[i] = f[i] * t_second
                    e[i + hb] = e[i + hb] * t_first
                    t[i] = both
                    t[i + hb] = both
    levels.append((list(e), list(f)))
    return levels, t[0]


def _hgrn_kernel(x_ref, g_ref, win_ref, lb_ref, gn_ref, wout_ref, o_ref,
                 state_ref, q_ref, gate_ref, v_ref, sg_ref, y_ref, *, layer):
    tm, d = x_ref.shape[1], x_ref.shape[2]

    @pl.when(pl.program_id(1) == 0)
    def _():
        state_ref[...] = jnp.zeros_like(state_ref)

    lbr = lb_ref[...]
    e = jnp.exp(lbr - jnp.max(lbr, axis=0, keepdims=True))
    sm = e / jnp.sum(e, axis=0, keepdims=True)
    cum = sm[0:1, :]
    for i in range(1, layer + 1):
        cum = cum + sm[i:i + 1, :]
    lb = cum - sm[0:1, :]

    x = x_ref[0]
    h = _rmsnorm(x, g_ref[...]).astype(BF16)
    qf = _dot(h, win_ref[:, 0:d])
    q_ref[...] = qf * _sigmoid(qf)
    gate_ref[...] = lb + (1.0 - lb) * _sigmoid(_dot(h, win_ref[:, d:2 * d]))
    v_ref[...] = _dot(h, win_ref[:, 2 * d:3 * d]).astype(BF16)
    og = _dot(h, win_ref[:, 3 * d:4 * d])
    sg_ref[...] = og * _sigmoid(og)

    t_io = lax.broadcasted_iota(jnp.int32, (CHUNK, CHUNK), 0)
    s_io = lax.broadcasted_iota(jnp.int32, (CHUNK, CHUNK), 1)
    diff = t_io ^ s_io
    pair_level = jnp.zeros((CHUNK, CHUNK), jnp.int32)
    for m in range(LOG2_CHUNK):
        pair_level = jnp.where((diff >> m) == 1, m + 1, pair_level)
    pair_level = jnp.where(s_io > t_io, -1, pair_level)

    def scaled(vals, factors):
        return jnp.concatenate([a * b for a, b in zip(vals, factors)], axis=0).astype(BF16)

    def scores(ci, n):
        rows = slice(ci * CHUNK, (ci + 1) * CHUNK)
        cs = slice(n * HG_DK, (n + 1) * HG_DK)
        qv = _split_rows(q_ref[rows, cs])
        gv = _split_rows(gate_ref[rows, cs])
        kv = [1.0 - g for g in gv]
        levels, chunk_decay = _decay_levels(gv)
        diag = jnp.concatenate([jnp.sum(q * k, axis=-1, keepdims=True) for q, k in zip(qv, kv)], axis=0)
        pair = jnp.concatenate([jnp.sum(q * g * pltpu.roll(k, 1, 0), axis=-1, keepdims=True)
                                for q, g, k in zip(qv, gv, kv)], axis=0)
        attn = jnp.where(pair_level == 0, diag, jnp.where(pair_level == 1, pair, 0.0))
        for m in range(1, LOG2_CHUNK):
            e_m, f_m = levels[m]
            attn = jnp.where(pair_level == m + 1, _dot_nt(scaled(qv, e_m), scaled(kv, f_m)), attn)
        e_c, f_c = levels[LOG2_CHUNK]
        decay_rows = jnp.transpose(jnp.broadcast_to(chunk_decay[0:1, :], (HG_DK, HG_DK)))
        return attn.astype(BF16), scaled(qv, e_c), scaled(kv, f_c), decay_rows

    def recur(ci, n, attn, q_c, k_c, decay_rows):
        rows = slice(ci * CHUNK, (ci + 1) * CHUNK)
        cs = slice(n * HG_DK, (n + 1) * HG_DK)
        vch = v_ref[rows, cs]
        st = state_ref[n]
        o = _dot(jnp.concatenate([q_c, attn], axis=1), jnp.concatenate([st.astype(BF16), vch], axis=0))
        state_ref[n] = st * decay_rows + _dot_tn(k_c, vch)
        on = _rmsnorm(o, gn_ref[:, cs])
        y_ref[rows, cs] = (on * sg_ref[rows, cs]).astype(BF16)

    pending = []
    for ci in range(tm // CHUNK):
        for n in range(HG_HEADS):
            pending.append((ci, n, scores(ci, n)))
            if len(pending) > SCORE_LOOKAHEAD:
                pci, pn, vals = pending.pop(0)
                recur(pci, pn, *vals)
    for pci, pn, vals in pending:
        recur(pci, pn, *vals)
    o_ref[0] = x + _dot(y_ref[...], wout_ref[...])


def _hgrn_mixer(x, g, w_in, lb_raw, g_norm, w_out, *, layer, mixer, tm=512):
    b, s, d = x.shape
    n_layers = lb_raw.shape[0]
    return pl.pallas_call(
        functools.partial(_hgrn_kernel, layer=layer),
        grid=(b, s // tm),
        in_specs=[pl.BlockSpec((1, tm, d), lambda i, j: (i, j, 0)),
                  _layer(g, layer),
                  _layer(w_in, mixer),
                  _resident((n_layers, d)),
                  _layer(g_norm, mixer),
                  _layer(w_out, mixer)],
        out_specs=pl.BlockSpec((1, tm, d), lambda i, j: (i, j, 0)),
        out_shape=jax.ShapeDtypeStruct((b, s, d), F32),
        scratch_shapes=[pltpu.VMEM((HG_HEADS, HG_DK, HG_DK), F32),
                        pltpu.VMEM((tm, d), F32),
                        pltpu.VMEM((tm, d), F32),
                        pltpu.VMEM((tm, d), BF16),
                        pltpu.VMEM((tm, d), F32),
                        pltpu.VMEM((tm, d), BF16)],
        compiler_params=_params(2),
        name="hgrn_mixer",
    )(x, g, w_in, lb_raw, g_norm, w_out)


def kernel(x, mem, norm_mix, conv_w_in, conv_w, conv_w_out, hgrn_w_in, hgrn_w_out, hgrn_norm, hgrn_lb,
           norm_xattn, norm_mem, xattn_w_q, xattn_w_kv, xattn_w_o, norm_ffn, ffn_w_in, ffn_w_out, final_norm):
    b, s, d = x.shape
    depth = norm_mix.shape[0]
    n_mixers = 2
    rows = lambda g: g.reshape(g.shape[0], 1, d)
    norm_mix, norm_xattn, norm_mem, norm_ffn, hgrn_norm = map(rows, (norm_mix, norm_xattn, norm_mem, norm_ffn,
                                                                    hgrn_norm))
    final_norm = final_norm.reshape(1, d)
    for i in range(depth):
        j = i // n_mixers
        if i % n_mixers == 0:
            x = _conv_mixer(x, norm_mix, conv_w_in, conv_w, conv_w_out, layer=i, mixer=j)
        else:
            x = _hgrn_mixer(x, norm_mix, hgrn_w_in, hgrn_lb, hgrn_norm, hgrn_w_out, layer=i, mixer=j)
        sw, vw = _mem_proj(mem, norm_mem, xattn_w_kv, xattn_w_q, xattn_w_o, layer=i)
        x = _xattn(x, norm_xattn, sw, vw, layer=i)
        x = _ffn(x.reshape(b * s, d), norm_ffn, ffn_w_in, ffn_w_out, final_norm, layer=i,
                 final_norm=(i == depth - 1)).reshape(b, s, d)
    return x
```

```python
import functools

import jax
import jax.numpy as jnp
from jax import lax
from jax.experimental import pallas as pl
from jax.experimental.pallas import tpu as pltpu

EPS = 1e-6
CHUNK = 64
LOG2_CHUNK = 6
HG_HEADS = 8
HG_DK = 128
X_HEADS = 4
X_SUBTILES = 2
CONV_W = 3
SUBLANES = 8
VMEM_LIMIT_BYTES = 56 * 1024 * 1024

F32 = jnp.float32
BF16 = jnp.bfloat16


def _rmsnorm(x, g):
    ms = jnp.mean(x * x, axis=-1, keepdims=True)
    return x * lax.rsqrt(ms + EPS) * g


def _sigmoid(x):
    return 1.0 / (1.0 + jnp.exp(-x))


def _dot(a, b):
    return jnp.dot(a.astype(BF16), b.astype(BF16), preferred_element_type=F32)


def _dot_nt(a, b):
    return lax.dot_general(a.astype(BF16), b.astype(BF16), (((1,), (1,)), ((), ())), preferred_element_type=F32)


def _dot_tn(a, b):
    return lax.dot_general(a.astype(BF16), b.astype(BF16), (((0,), (0,)), ((), ())), preferred_element_type=F32)


def _params(n_axes):
    return pltpu.CompilerParams(dimension_semantics=("arbitrary",) * n_axes,
                                vmem_limit_bytes=VMEM_LIMIT_BYTES)


def _resident(shape):
    return pl.BlockSpec(shape, lambda *_: (0,) * len(shape), pipeline_mode=pl.Buffered(1))


def _layer(stacked, layer):
    shape = stacked.shape[1:]
    return pl.BlockSpec((None,) + shape, lambda *_: (layer,) + (0,) * len(shape), pipeline_mode=pl.Buffered(1))


def _mem_kernel(mem_ref, g_ref, wkv_ref, wq_ref, wo_ref, sw_ref, vw_ref):
    m, d = mem_ref.shape[1], mem_ref.shape[2]
    hd = d // X_HEADS
    h = _rmsnorm(mem_ref[0], g_ref[...])
    kv = _dot(h, wkv_ref[...])
    k, v = kv[:, :d].astype(BF16), kv[:, d:].astype(BF16)
    for n in range(X_HEADS):
        cs = slice(n * hd, (n + 1) * hd)
        sw_ref[0, :, n * m:(n + 1) * m] = (_dot_nt(wq_ref[:, cs], k[:, cs]) * (hd ** -0.5)).astype(BF16)
        vw_ref[0, n * m:(n + 1) * m, :] = _dot(v[:, cs], wo_ref[cs, :]).astype(BF16)


def _mem_proj(mem, g, w_kv, w_q, w_o, *, layer):
    b, m, d = mem.shape
    hm = X_HEADS * m
    return pl.pallas_call(
        _mem_kernel,
        grid=(b,),
        in_specs=[pl.BlockSpec((1, m, d), lambda i: (i, 0, 0)),
                  _layer(g, layer),
                  _layer(w_kv, layer),
                  _layer(w_q, layer),
                  _layer(w_o, layer)],
        out_specs=[pl.BlockSpec((1, d, hm), lambda i: (i, 0, 0)),
                   pl.BlockSpec((1, hm, d), lambda i: (i, 0, 0))],
        out_shape=[jax.ShapeDtypeStruct((b, d, hm), BF16), jax.ShapeDtypeStruct((b, hm, d), BF16)],
        compiler_params=_params(1),
        name="mem_proj",
    )(mem, g, w_kv, w_q, w_o)


def _conv_kernel(x_ref, g_ref, win_ref, cw_ref, wout_ref, o_ref, carry_ref, *, nc):
    tm, d = x_ref.shape[1], x_ref.shape[2]

    @pl.when(pl.program_id(1) == 0)
    def _():
        carry_ref[...] = jnp.zeros_like(carry_ref)

    x = x_ref[0]
    h = _rmsnorm(x, g_ref[...]).astype(BF16)

    def project(c):
        return [_dot(h, win_ref[:, part * d + c * nc:part * d + (c + 1) * nc]) for part in range(3)]

    def gated_conv(c, gb, gc, u):
        cs = slice(c * nc, (c + 1) * nc)
        u2 = gc * u
        ext = jnp.concatenate([carry_ref[:, cs], u2], axis=0)
        s1 = pltpu.roll(ext, 1, 0)[SUBLANES:]
        s2 = pltpu.roll(ext, 2, 0)[SUBLANES:]
        z = cw_ref[0:1, cs] * s2 + cw_ref[1:2, cs] * s1 + cw_ref[2:3, cs] * u2
        carry_ref[:, cs] = u2[tm - SUBLANES:, :]
        return (gb * z).astype(BF16)

    n_chunks = d // nc
    acc = x
    proj = project(0)
    for c in range(n_chunks):
        nxt = project(c + 1) if c + 1 < n_chunks else None
        acc = acc + _dot(gated_conv(c, *proj), wout_ref[c * nc:(c + 1) * nc, :])
        proj = nxt
    o_ref[0] = acc


def _conv_mixer(x, g, w_in, cw, w_out, *, layer, mixer, tm=1024, nc=256):
    b, s, d = x.shape
    return pl.pallas_call(
        functools.partial(_conv_kernel, nc=nc),
        grid=(b, s // tm),
        in_specs=[pl.BlockSpec((1, tm, d), lambda i, j: (i, j, 0)),
                  _layer(g, layer),
                  _layer(w_in, mixer),
                  _layer(cw, mixer),
                  _layer(w_out, mixer)],
        out_specs=pl.BlockSpec((1, tm, d), lambda i, j: (i, j, 0)),
        out_shape=jax.ShapeDtypeStruct((b, s, d), F32),
        scratch_shapes=[pltpu.VMEM((SUBLANES, d), F32)],
        compiler_params=_params(2),
        name="conv_mixer",
    )(x, g, w_in, cw, w_out)


def _xattn_kernel(x_ref, g_ref, sw_ref, vw_ref, o_ref):
    tm = x_ref.shape[1]
    m = sw_ref.shape[2] // X_HEADS
    ts = tm // X_SUBTILES
    subtiles = [slice(r * ts, (r + 1) * ts) for r in range(X_SUBTILES)]
    xs = [x_ref[0, rs, :] for rs in subtiles]
    scores = [_dot(_rmsnorm(x, g_ref[...]), sw_ref[0]) for x in xs]
    for rs, x, sc_r in zip(subtiles, xs, scores):
        probs = []
        for n in range(X_HEADS):
            sc = sc_r[:, n * m:(n + 1) * m]
            e = jnp.exp(sc - jnp.max(sc, axis=-1, keepdims=True))
            probs.append((e / jnp.sum(e, axis=-1, keepdims=True)).astype(BF16))
        o_ref[0, rs, :] = x + _dot(jnp.concatenate(probs, axis=-1), vw_ref[0])


def _xattn(x, g, sw, vw, *, layer, tm=1024):
    b, s, d = x.shape
    hm = sw.shape[2]
    return pl.pallas_call(
        _xattn_kernel,
        grid=(b, s // tm),
        in_specs=[pl.BlockSpec((1, tm, d), lambda i, j: (i, j, 0)),
                  _layer(g, layer),
                  pl.BlockSpec((1, d, hm), lambda i, j: (i, 0, 0)),
                  pl.BlockSpec((1, hm, d), lambda i, j: (i, 0, 0))],
        out_specs=pl.BlockSpec((1, tm, d), lambda i, j: (i, j, 0)),
        out_shape=jax.ShapeDtypeStruct((b, s, d), F32),
        compiler_params=_params(2),
        name="xattn",
    )(x, g, sw, vw)


def _ffn_kernel(x_ref, g_ref, win_ref, wout_ref, gf_ref, o_ref, *, nc, final_norm):
    d_ff = wout_ref.shape[0]
    x = x_ref[...]
    h = _rmsnorm(x, g_ref[...]).astype(BF16)

    def up(c):
        return (_dot(h, win_ref[:, c * nc:(c + 1) * nc]),
                _dot(h, win_ref[:, d_ff + c * nc:d_ff + (c + 1) * nc]))

    n_chunks = d_ff // nc
    acc = x
    ab = up(0)
    for c in range(n_chunks):
        nxt = up(c + 1) if c + 1 < n_chunks else None
        a, b = ab
        act = (a * _sigmoid(a)) * b
        acc = acc + _dot(act, wout_ref[c * nc:(c + 1) * nc, :])
        ab = nxt
    if final_norm:
        acc = _rmsnorm(acc, gf_ref[...])
    o_ref[...] = acc


def _ffn(x2d, g, w_in, w_out, g_final, *, layer, final_norm, tm=512, nc=256):
    t, d = x2d.shape
    return pl.pallas_call(
        functools.partial(_ffn_kernel, nc=nc, final_norm=final_norm),
        grid=(t // tm,),
        in_specs=[pl.BlockSpec((tm, d), lambda i: (i, 0)),
                  _layer(g, layer),
                  _layer(w_in, layer),
                  _layer(w_out, layer),
                  _resident((1, d))],
        out_specs=pl.BlockSpec((tm, d), lambda i: (i, 0)),
        out_shape=jax.ShapeDtypeStruct((t, d), F32),
        compiler_params=_params(1),
        name="ffn",
    )(x2d, g, w_in, w_out, g_final)


SCORE_LOOKAHEAD = 2


def _split_rows(a):
    return [a[SUBLANES * i:SUBLANES * (i + 1)] for i in range(a.shape[0] // SUBLANES)]


def _decay_levels(gv):
    nv = len(gv)
    sub = lax.broadcasted_iota(jnp.int32, gv[0].shape, 0)
    e, f, t = list(gv), [jnp.ones_like(gv[0])] * nv, list(gv)
    levels = []
    for m in range(LOG2_CHUNK):
        levels.append((list(e), list(f)))
        half = 1 << m
        if half < SUBLANES:
            second = ((sub >> m) & 1) == 1
            for i in range(nv):
                t_prev = pltpu.roll(t[i], half, 0)
                t_next = pltpu.roll(t[i], SUBLANES - half, 0)
                e[i] = jnp.where(second, e[i] * t_prev, e[i])
                f[i] = jnp.where(second, f[i], f[i] * t_next)
                t[i] = t[i] * jnp.where(second, t_prev, t_next)
        else:
            hb = half // SUBLANES
            for blk in range(0, nv, 2 * hb):
                t_first, t_second = t[blk], t[blk + hb]
                both = t_first * t_second
                for i in range(blk, blk + hb):
                    f[i] = f[i] * t_second
                    e[i + hb] = e[i + hb] * t_first
                    t[i] = both
                    t[i + hb] = both
    levels.append((list(e), list(f)))
    return levels, t[0]


def _hgrn_kernel(x_ref, g_ref, win_ref, lb_ref, gn_ref, wout_ref, o_ref,
                 state_ref, q_ref, gate_ref, v_ref, sg_ref, y_ref, *, layer):
    tm, d = x_ref.shape[1], x_ref.shape[2]

    @pl.when(pl.program_id(1) == 0)
    def _():
        state_ref[...] = jnp.zeros_like(state_ref)

    lbr = lb_ref[...]
    e = jnp.exp(lbr - jnp.max(lbr, axis=0, keepdims=True))
    sm = e / jnp.sum(e, axis=0, keepdims=True)
    cum = sm[0:1, :]
    for i in range(1, layer + 1):
        cum = cum + sm[i:i + 1, :]
    lb = cum - sm[0:1, :]

    x = x_ref[0]
    h = _rmsnorm(x, g_ref[...]).astype(BF16)
    qf = _dot(h, win_ref[:, 0:d])
    q_ref[...] = qf * _sigmoid(qf)
    gate_ref[...] = lb + (1.0 - lb) * _sigmoid(_dot(h, win_ref[:, d:2 * d]))
    v_ref[...] = _dot(h, win_ref[:, 2 * d:3 * d]).astype(BF16)
    og = _dot(h, win_ref[:, 3 * d:4 * d])
    sg_ref[...] = og * _sigmoid(og)

    t_io = lax.broadcasted_iota(jnp.int32, (CHUNK, CHUNK), 0)
    s_io = lax.broadcasted_iota(jnp.int32, (CHUNK, CHUNK), 1)
    diff = t_io ^ s_io
    pair_level = jnp.zeros((CHUNK, CHUNK), jnp.int32)
    for m in range(LOG2_CHUNK):
        pair_level = jnp.where((diff >> m) == 1, m + 1, pair_level)
    level_blocks = _split_rows(jnp.where(s_io > t_io, -1, pair_level))

    def scaled(vals, factors):
        return jnp.concatenate([a * b for a, b in zip(vals, factors)], axis=0).astype(BF16)

    def scores(ci, n):
        rows = slice(ci * CHUNK, (ci + 1) * CHUNK)
        cs = slice(n * HG_DK, (n + 1) * HG_DK)
        qv = _split_rows(q_ref[rows, cs])
        gv = _split_rows(gate_ref[rows, cs])
        kv = [1.0 - g for g in gv]
        levels, chunk_decay = _decay_levels(gv)
        attn = []
        for q, g, k, lv in zip(qv, gv, kv, level_blocks):
            diag = jnp.sum(q * k, axis=-1, keepdims=True)
            pair = jnp.sum(q * g * pltpu.roll(k, 1, 0), axis=-1, keepdims=True)
            attn.append(jnp.where(lv == 0, diag, jnp.where(lv == 1, pair, 0.0)))
        for m in range(1, LOG2_CHUNK):
            e_m, f_m = levels[m]
            half_blocks = (1 << m) // SUBLANES
            if half_blocks == 0:
                t_blocks = list(range(len(qv)))
                a = _dot_nt(scaled(qv, e_m), scaled(kv, f_m))
            else:
                second = [(i // half_blocks) % 2 == 1 for i in range(len(qv))]
                t_blocks = [i for i, s in enumerate(second) if s]
                keys = [jnp.zeros_like(k) if s else k * f for k, f, s in zip(kv, f_m, second)]
                a = _dot_nt(scaled([qv[i] for i in t_blocks], [e_m[i] for i in t_blocks]),
                            jnp.concatenate(keys, axis=0))
            for a_blk, i in zip(_split_rows(a), t_blocks):
                attn[i] = jnp.where(level_blocks[i] == m + 1, a_blk, attn[i])
        e_c, f_c = levels[LOG2_CHUNK]
        return jnp.concatenate(attn, axis=0).astype(BF16), scaled(qv, e_c), scaled(kv, f_c), chunk_decay

    def recur(ci, n, attn, q_c, k_c, chunk_decay):
        rows = slice(ci * CHUNK, (ci + 1) * CHUNK)
        cs = slice(n * HG_DK, (n + 1) * HG_DK)
        vch = v_ref[rows, cs]
        st = state_ref[n]
        decay_rows = jnp.transpose(jnp.broadcast_to(chunk_decay[0:1, :], (HG_DK, HG_DK)))
        o = _dot(jnp.concatenate([q_c, attn], axis=1), jnp.concatenate([st.astype(BF16), vch], axis=0))
        state_ref[n] = st * decay_rows + _dot_tn(k_c, vch)
        on = _rmsnorm(o, gn_ref[:, cs])
        y_ref[rows, cs] = (on * sg_ref[rows, cs]).astype(BF16)

    pending = []
    for ci in range(tm // CHUNK):
        for n in range(HG_HEADS):
            pending.append((ci, n, scores(ci, n)))
            if len(pending) > SCORE_LOOKAHEAD:
                pci, pn, vals = pending.pop(0)
                recur(pci, pn, *vals)
    for pci, pn, vals in pending:
        recur(pci, pn, *vals)
    o_ref[0] = x + _dot(y_ref[...], wout_ref[...])


def _hgrn_mixer(x, g, w_in, lb_raw, g_norm, w_out, *, layer, mixer, tm=512):
    b, s, d = x.shape
    n_layers = lb_raw.shape[0]
    return pl.pallas_call(
        functools.partial(_hgrn_kernel, layer=layer),
        grid=(b, s // tm),
        in_specs=[pl.BlockSpec((1, tm, d), lambda i, j: (i, j, 0)),
                  _layer(g, layer),
                  _layer(w_in, mixer),
                  _resident((n_layers, d)),
                  _layer(g_norm, mixer),
                  _layer(w_out, mixer)],
        out_specs=pl.BlockSpec((1, tm, d), lambda i, j: (i, j, 0)),
        out_shape=jax.ShapeDtypeStruct((b, s, d), F32),
        scratch_shapes=[pltpu.VMEM((HG_HEADS, HG_DK, HG_DK), F32),
                        pltpu.VMEM((tm, d), F32),
                        pltpu.VMEM((tm, d), F32),
                        pltpu.VMEM((tm, d), BF16),
                        pltpu.VMEM((tm, d), F32),
                        pltpu.VMEM((tm, d), BF16)],
        compiler_params=_params(2),
        name="hgrn_mixer",
    )(x, g, w_in, lb_raw, g_norm, w_out)


def kernel(x, mem, norm_mix, conv_w_in, conv_w, conv_w_out, hgrn_w_in, hgrn_w_out, hgrn_norm, hgrn_lb,
           norm_xattn, norm_mem, xattn_w_q, xattn_w_kv, xattn_w_o, norm_ffn, ffn_w_in, ffn_w_out, final_norm):
    b, s, d = x.shape
    depth = norm_mix.shape[0]
    n_mixers = 2
    rows = lambda g: g.reshape(g.shape[0], 1, d)
    norm_mix, norm_xattn, norm_mem, norm_ffn, hgrn_norm = map(rows, (norm_mix, norm_xattn, norm_mem, norm_ffn,
                                                                    hgrn_norm))
    final_norm = final_norm.reshape(1, d)
    for i in range(depth):
        j = i // n_mixers
        if i % n_mixers == 0:
            x = _conv_mixer(x, norm_mix, conv_w_in, conv_w, conv_w_out, layer=i, mixer=j)
        else:
            x = _hgrn_mixer(x, norm_mix, hgrn_w_in, hgrn_lb, hgrn_norm, hgrn_w_out, layer=i, mixer=j)
        sw, vw = _mem_proj(mem, norm_mem, xattn_w_kv, xattn_w_q, xattn_w_o, layer=i)
        x = _xattn(x, norm_xattn, sw, vw, layer=i)
        x = _ffn(x.reshape(b * s, d), norm_ffn, ffn_w_in, ffn_w_out, final_norm, layer=i,
                 final_norm=(i == depth - 1)).reshape(b, s, d)
    return x
```

```python
import functools

import jax
import jax.numpy as jnp
from jax import lax
from jax.experimental import pallas as pl
from jax.experimental.pallas import tpu as pltpu

EPS = 1e-6
CHUNK = 64
LOG2_CHUNK = 6
HG_HEADS = 8
HG_DK = 128
X_HEADS = 4
X_SUBTILES = 2
CONV_W = 3
SUBLANES = 8
VMEM_LIMIT_BYTES = 56 * 1024 * 1024

F32 = jnp.float32
BF16 = jnp.bfloat16


def _rmsnorm(x, g):
    ms = jnp.mean(x * x, axis=-1, keepdims=True)
    return x * lax.rsqrt(ms + EPS) * g


def _sigmoid(x):
    return 1.0 / (1.0 + jnp.exp(-x))


def _dot(a, b):
    return jnp.dot(a.astype(BF16), b.astype(BF16), preferred_element_type=F32)


def _dot_nt(a, b):
    return lax.dot_general(a.astype(BF16), b.astype(BF16), (((1,), (1,)), ((), ())), preferred_element_type=F32)


def _dot_tn(a, b):
    return lax.dot_general(a.astype(BF16), b.astype(BF16), (((0,), (0,)), ((), ())), preferred_element_type=F32)


def _params(n_axes):
    return pltpu.CompilerParams(dimension_semantics=("arbitrary",) * n_axes,
                                vmem_limit_bytes=VMEM_LIMIT_BYTES)


def _resident(shape):
    return pl.BlockSpec(shape, lambda *_: (0,) * len(shape), pipeline_mode=pl.Buffered(1))


def _layer(stacked, layer):
    shape = stacked.shape[1:]
    return pl.BlockSpec((None,) + shape, lambda *_: (layer,) + (0,) * len(shape), pipeline_mode=pl.Buffered(1))


def _mem_kernel(mem_ref, g_ref, wkv_ref, wq_ref, wo_ref, sw_ref, vw_ref):
    m, d = mem_ref.shape[1], mem_ref.shape[2]
    hd = d // X_HEADS
    h = _rmsnorm(mem_ref[0], g_ref[...])
    kv = _dot(h, wkv_ref[...])
    k, v = kv[:, :d].astype(BF16), kv[:, d:].astype(BF16)
    for n in range(X_HEADS):
        cs = slice(n * hd, (n + 1) * hd)
        sw_ref[0, :, n * m:(n + 1) * m] = (_dot_nt(wq_ref[:, cs], k[:, cs]) * (hd ** -0.5)).astype(BF16)
        vw_ref[0, n * m:(n + 1) * m, :] = _dot(v[:, cs], wo_ref[cs, :]).astype(BF16)


def _mem_proj(mem, g, w_kv, w_q, w_o, *, layer):
    b, m, d = mem.shape
    hm = X_HEADS * m
    return pl.pallas_call(
        _mem_kernel,
        grid=(b,),
        in_specs=[pl.BlockSpec((1, m, d), lambda i: (i, 0, 0)),
                  _layer(g, layer),
                  _layer(w_kv, layer),
                  _layer(w_q, layer),
                  _layer(w_o, layer)],
        out_specs=[pl.BlockSpec((1, d, hm), lambda i: (i, 0, 0)),
                   pl.BlockSpec((1, hm, d), lambda i: (i, 0, 0))],
        out_shape=[jax.ShapeDtypeStruct((b, d, hm), BF16), jax.ShapeDtypeStruct((b, hm, d), BF16)],
        compiler_params=_params(1),
        name="mem_proj",
    )(mem, g, w_kv, w_q, w_o)


def _conv_kernel(x_ref, g_ref, win_ref, cw_ref, wout_ref, o_ref, carry_ref, *, nc):
    tm, d = x_ref.shape[1], x_ref.shape[2]

    @pl.when(pl.program_id(1) == 0)
    def _():
        carry_ref[...] = jnp.zeros_like(carry_ref)

    x = x_ref[0]
    h = _rmsnorm(x, g_ref[...]).astype(BF16)

    def project(c):
        return [_dot(h, win_ref[:, part * d + c * nc:part * d + (c + 1) * nc]) for part in range(3)]

    def gated_conv(c, gb, gc, u):
        cs = slice(c * nc, (c + 1) * nc)
        u2 = gc * u
        ext = jnp.concatenate([carry_ref[:, cs], u2], axis=0)
        s1 = pltpu.roll(ext, 1, 0)[SUBLANES:]
        s2 = pltpu.roll(ext, 2, 0)[SUBLANES:]
        z = cw_ref[0:1, cs] * s2 + cw_ref[1:2, cs] * s1 + cw_ref[2:3, cs] * u2
        carry_ref[:, cs] = u2[tm - SUBLANES:, :]
        return (gb * z).astype(BF16)

    n_chunks = d // nc
    acc = x
    proj = project(0)
    for c in range(n_chunks):
        nxt = project(c + 1) if c + 1 < n_chunks else None
        acc = acc + _dot(gated_conv(c, *proj), wout_ref[c * nc:(c + 1) * nc, :])
        proj = nxt
    o_ref[0] = acc


def _conv_mixer(x, g, w_in, cw, w_out, *, layer, mixer, tm=1024, nc=256):
    b, s, d = x.shape
    return pl.pallas_call(
        functools.partial(_conv_kernel, nc=nc),
        grid=(b, s // tm),
        in_specs=[pl.BlockSpec((1, tm, d), lambda i, j: (i, j, 0)),
                  _layer(g, layer),
                  _layer(w_in, mixer),
                  _layer(cw, mixer),
                  _layer(w_out, mixer)],
        out_specs=pl.BlockSpec((1, tm, d), lambda i, j: (i, j, 0)),
        out_shape=jax.ShapeDtypeStruct((b, s, d), F32),
        scratch_shapes=[pltpu.VMEM((SUBLANES, d), F32)],
        compiler_params=_params(2),
        name="conv_mixer",
    )(x, g, w_in, cw, w_out)


def _xattn_kernel(x_ref, g_ref, sw_ref, vw_ref, o_ref):
    tm = x_ref.shape[1]
    m = sw_ref.shape[2] // X_HEADS
    ts = tm // X_SUBTILES
    subtiles = [slice(r * ts, (r + 1) * ts) for r in range(X_SUBTILES)]
    xs = [x_ref[0, rs, :] for rs in subtiles]
    scores = [_dot(_rmsnorm(x, g_ref[...]), sw_ref[0]) for x in xs]
    for rs, x, sc_r in zip(subtiles, xs, scores):
        probs = []
        for n in range(X_HEADS):
            sc = sc_r[:, n * m:(n + 1) * m]
            e = jnp.exp(sc - jnp.max(sc, axis=-1, keepdims=True))
            probs.append((e / jnp.sum(e, axis=-1, keepdims=True)).astype(BF16))
        o_ref[0, rs, :] = x + _dot(jnp.concatenate(probs, axis=-1), vw_ref[0])


def _xattn(x, g, sw, vw, *, layer, tm=1024):
    b, s, d = x.shape
    hm = sw.shape[2]
    return pl.pallas_call(
        _xattn_kernel,
        grid=(b, s // tm),
        in_specs=[pl.BlockSpec((1, tm, d), lambda i, j: (i, j, 0)),
                  _layer(g, layer),
                  pl.BlockSpec((1, d, hm), lambda i, j: (i, 0, 0)),
                  pl.BlockSpec((1, hm, d), lambda i, j: (i, 0, 0))],
        out_specs=pl.BlockSpec((1, tm, d), lambda i, j: (i, j, 0)),
        out_shape=jax.ShapeDtypeStruct((b, s, d), F32),
        compiler_params=_params(2),
        name="xattn",
    )(x, g, sw, vw)


def _ffn_kernel(x_ref, g_ref, win_ref, wout_ref, gf_ref, o_ref, *, nc, final_norm):
    d_ff = wout_ref.shape[0]
    x = x_ref[...]
    h = _rmsnorm(x, g_ref[...]).astype(BF16)

    def up(c):
        return (_dot(h, win_ref[:, c * nc:(c + 1) * nc]),
                _dot(h, win_ref[:, d_ff + c * nc:d_ff + (c + 1) * nc]))

    n_chunks = d_ff // nc
    acc = x
    ab = up(0)
    for c in range(n_chunks):
        nxt = up(c + 1) if c + 1 < n_chunks else None
        a, b = ab
        act = (a * _sigmoid(a)) * b
        acc = acc + _dot(act, wout_ref[c * nc:(c + 1) * nc, :])
        ab = nxt
    if final_norm:
        acc = _rmsnorm(acc, gf_ref[...])
    o_ref[...] = acc


def _ffn(x2d, g, w_in, w_out, g_final, *, layer, final_norm, tm=512, nc=256):
    t, d = x2d.shape
    return pl.pallas_call(
        functools.partial(_ffn_kernel, nc=nc, final_norm=final_norm),
        grid=(t // tm,),
        in_specs=[pl.BlockSpec((tm, d), lambda i: (i, 0)),
                  _layer(g, layer),
                  _layer(w_in, layer),
                  _layer(w_out, layer),
                  _resident((1, d))],
        out_specs=pl.BlockSpec((tm, d), lambda i: (i, 0)),
        out_shape=jax.ShapeDtypeStruct((t, d), F32),
        compiler_params=_params(1),
        name="ffn",
    )(x2d, g, w_in, w_out, g_final)


SCORE_LOOKAHEAD = 2


def _split_rows(a):
    return [a[SUBLANES * i:SUBLANES * (i + 1)] for i in range(a.shape[0] // SUBLANES)]


def _decay_levels(gv):
    nv = len(gv)
    sub = lax.broadcasted_iota(jnp.int32, gv[0].shape, 0)
    e, f, t = list(gv), [jnp.ones_like(gv[0])] * nv, list(gv)
    levels = []
    for m in range(LOG2_CHUNK):
        levels.append((list(e), list(f)))
        half = 1 << m
        if half < SUBLANES:
            second = ((sub >> m) & 1) == 1
            for i in range(nv):
                t_prev = pltpu.roll(t[i], half, 0)
                t_next = pltpu.roll(t[i], SUBLANES - half, 0)
                e[i] = jnp.where(second, e[i] * t_prev, e[i])
                f[i] = jnp.where(second, f[i], f[i] * t_next)
                t[i] = t[i] * jnp.where(second, t_prev, t_next)
        else:
            hb = half // SUBLANES
            for blk in range(0, nv, 2 * hb):
                t_first, t_second = t[blk], t[blk + hb]
                both = t_first * t_second
                for i in range(blk, blk + hb):
                    f[i] = f[i] * t_second
                    e[i + hb] = e[i + hb] * t_first
                    t[i] = both
                    t[i + hb] = both
    levels.append((list(e), list(f)))
    return levels, t[0]


def _hgrn_kernel(x_ref, g_ref, win_ref, lb_ref, gn_ref, wout_ref, o_ref,
                 state_ref, q_ref, gate_ref, v_ref, sg_ref, y_ref, *, layer):
    tm, d = x_ref.shape[1], x_ref.shape[2]

    @pl.when(pl.program_id(1) == 0)
    def _():
        state_ref[...] = jnp.zeros_like(state_ref)

    lbr = lb_ref[...]
    e = jnp.exp(lbr - jnp.max(lbr, axis=0, keepdims=True))
    sm = e / jnp.sum(e, axis=0, keepdims=True)
    cum = sm[0:1, :]
    for i in range(1, layer + 1):
        cum = cum + sm[i:i + 1, :]
    lb = cum - sm[0:1, :]

    x = x_ref[0]
    h = _rmsnorm(x, g_ref[...]).astype(BF16)
    gate_ref[...] = lb + (1.0 - lb) * _sigmoid(_dot(h, win_ref[:, d:2 * d]))
    qf = _dot(h, win_ref[:, 0:d])
    q_ref[...] = qf * _sigmoid(qf)
    v_ref[...] = _dot(h, win_ref[:, 2 * d:3 * d]).astype(BF16)
    og = _dot(h, win_ref[:, 3 * d:4 * d])
    sg_ref[...] = og * _sigmoid(og)

    t_io = lax.broadcasted_iota(jnp.int32, (CHUNK, CHUNK), 0)
    s_io = lax.broadcasted_iota(jnp.int32, (CHUNK, CHUNK), 1)
    diff = t_io ^ s_io
    pair_level = jnp.zeros((CHUNK, CHUNK), jnp.int32)
    for m in range(LOG2_CHUNK):
        pair_level = jnp.where((diff >> m) == 1, m + 1, pair_level)
    level_blocks = _split_rows(jnp.where(s_io > t_io, -1, pair_level))

    def scaled(vals, factors):
        return jnp.concatenate([a * b for a, b in zip(vals, factors)], axis=0).astype(BF16)

    def scores(ci, n):
        rows = slice(ci * CHUNK, (ci + 1) * CHUNK)
        cs = slice(n * HG_DK, (n + 1) * HG_DK)
        qv = _split_rows(q_ref[rows, cs])
        gv = _split_rows(gate_ref[rows, cs])
        kv = [1.0 - g for g in gv]
        levels, chunk_decay = _decay_levels(gv)
        attn = []
        for q, g, k, lv in zip(qv, gv, kv, level_blocks):
            diag = jnp.sum(q * k, axis=-1, keepdims=True)
            pair = jnp.sum(q * g * pltpu.roll(k, 1, 0), axis=-1, keepdims=True)
            attn.append(jnp.where(lv == 0, diag, jnp.where(lv == 1, pair, 0.0)))
        for m in range(1, LOG2_CHUNK):
            e_m, f_m = levels[m]
            half_blocks = (1 << m) // SUBLANES
            if half_blocks == 0:
                t_blocks = list(range(len(qv)))
                a = _dot_nt(scaled(qv, e_m), scaled(kv, f_m))
            else:
                second = [(i // half_blocks) % 2 == 1 for i in range(len(qv))]
                t_blocks = [i for i, s in enumerate(second) if s]
                keys = [jnp.zeros_like(k) if s else k * f for k, f, s in zip(kv, f_m, second)]
                a = _dot_nt(scaled([qv[i] for i in t_blocks], [e_m[i] for i in t_blocks]),
                            jnp.concatenate(keys, axis=0))
            for a_blk, i in zip(_split_rows(a), t_blocks):
                attn[i] = jnp.where(level_blocks[i] == m + 1, a_blk, attn[i])
        e_c, f_c = levels[LOG2_CHUNK]
        return jnp.concatenate(attn, axis=0).astype(BF16), scaled(qv, e_c), scaled(kv, f_c), chunk_decay

    def recur(ci, n, attn, q_c, k_c, chunk_decay):
        rows = slice(ci * CHUNK, (ci + 1) * CHUNK)
        cs = slice(n * HG_DK, (n + 1) * HG_DK)
        vch = v_ref[rows, cs]
        st = state_ref[n]
        decay_rows = jnp.transpose(jnp.broadcast_to(chunk_decay[0:1, :], (HG_DK, HG_DK)))
        o = _dot(jnp.concatenate([q_c, attn], axis=1), jnp.concatenate([st.astype(BF16), vch], axis=0))
        state_ref[n] = st * decay_rows + _dot_tn(k_c, vch)
        on = _rmsnorm(o, gn_ref[:, cs])
        y_ref[rows, cs] = (on * sg_ref[rows, cs]).astype(BF16)

    pending = []
    for ci in range(tm // CHUNK):
        for n in range(HG_HEADS):
            pending.append((ci, n, scores(ci, n)))
            if len(pending) > SCORE_LOOKAHEAD:
                pci, pn, vals = pending.pop(0)
                recur(pci, pn, *vals)
    for pci, pn, vals in pending:
        recur(pci, pn, *vals)
    o_ref[0] = x + _dot(y_ref[...], wout_ref[...])


def _hgrn_mixer(x, g, w_in, lb_raw, g_norm, w_out, *, layer, mixer, tm=512):
    b, s, d = x.shape
    n_layers = lb_raw.shape[0]
    return pl.pallas_call(
        functools.partial(_hgrn_kernel, layer=layer),
        grid=(b, s // tm),
        in_specs=[pl.BlockSpec((1, tm, d), lambda i, j: (i, j, 0)),
                  _layer(g, layer),
                  _layer(w_in, mixer),
                  _resident((n_layers, d)),
                  _layer(g_norm, mixer),
                  _layer(w_out, mixer)],
        out_specs=pl.BlockSpec((1, tm, d), lambda i, j: (i, j, 0)),
        out_shape=jax.ShapeDtypeStruct((b, s, d), F32),
        scratch_shapes=[pltpu.VMEM((HG_HEADS, HG_DK, HG_DK), F32),
                        pltpu.VMEM((tm, d), F32),
                        pltpu.VMEM((tm, d), F32),
                        pltpu.VMEM((tm, d), BF16),
                        pltpu.VMEM((tm, d), F32),
                        pltpu.VMEM((tm, d), BF16)],
        compiler_params=_params(2),
        name="hgrn_mixer",
    )(x, g, w_in, lb_raw, g_norm, w_out)


def kernel(x, mem, norm_mix, conv_w_in, conv_w, conv_w_out, hgrn_w_in, hgrn_w_out, hgrn_norm, hgrn_lb,
           norm_xattn, norm_mem, xattn_w_q, xattn_w_kv, xattn_w_o, norm_ffn, ffn_w_in, ffn_w_out, final_norm):
    b, s, d = x.shape
    depth = norm_mix.shape[0]
    n_mixers = 2
    rows = lambda g: g.reshape(g.shape[0], 1, d)
    norm_mix, norm_xattn, norm_mem, norm_ffn, hgrn_norm = map(rows, (norm_mix, norm_xattn, norm_mem, norm_ffn,
                                                                    hgrn_norm))
    final_norm = final_norm.reshape(1, d)
    for i in range(depth):
        j = i // n_mixers
        if i % n_mixers == 0:
            x = _conv_mixer(x, norm_mix, conv_w_in, conv_w, conv_w_out, layer=i, mixer=j)
        else:
            x = _hgrn_mixer(x, norm_mix, hgrn_w_in, hgrn_lb, hgrn_norm, hgrn_w_out, layer=i, mixer=j)
        sw, vw = _mem_proj(mem, norm_mem, xattn_w_kv, xattn_w_q, xattn_w_o, layer=i)
        x = _xattn(x, norm_xattn, sw, vw, layer=i)
        x = _ffn(x.reshape(b * s, d), norm_ffn, ffn_w_in, ffn_w_out, final_norm, layer=i,
                 final_norm=(i == depth - 1)).reshape(b, s, d)
    return x
```

```python
import functools

import jax
import jax.numpy as jnp
from jax import lax
from jax.experimental import pallas as pl
from jax.experimental.pallas import tpu as pltpu

EPS = 1e-6
CHUNK = 64
LOG2_CHUNK = 6
HG_HEADS = 8
HG_DK = 128
X_HEADS = 4
X_SUBTILES = 2
CONV_W = 3
SUBLANES = 8
VMEM_LIMIT_BYTES = 56 * 1024 * 1024

F32 = jnp.float32
BF16 = jnp.bfloat16


def _rmsnorm(x, g):
    ms = jnp.mean(x * x, axis=-1, keepdims=True)
    return x * lax.rsqrt(ms + EPS) * g


def _sigmoid(x):
    return 1.0 / (1.0 + jnp.exp(-x))


def _dot(a, b):
    return jnp.dot(a.astype(BF16), b.astype(BF16), preferred_element_type=F32)


def _dot_nt(a, b):
    return lax.dot_general(a.astype(BF16), b.astype(BF16), (((1,), (1,)), ((), ())), preferred_element_type=F32)


def _dot_tn(a, b):
    return lax.dot_general(a.astype(BF16), b.astype(BF16), (((0,), (0,)), ((), ())), preferred_element_type=F32)


def _params(n_axes):
    return pltpu.CompilerParams(dimension_semantics=("arbitrary",) * n_axes,
                                vmem_limit_bytes=VMEM_LIMIT_BYTES)


def _resident(shape):
    return pl.BlockSpec(shape, lambda *_: (0,) * len(shape), pipeline_mode=pl.Buffered(1))


def _layer(stacked, layer):
    shape = stacked.shape[1:]
    return pl.BlockSpec((None,) + shape, lambda *_: (layer,) + (0,) * len(shape), pipeline_mode=pl.Buffered(1))


def _mem_kernel(mem_ref, g_ref, wkv_ref, wq_ref, wo_ref, sw_ref, vw_ref):
    m, d = mem_ref.shape[1], mem_ref.shape[2]
    hd = d // X_HEADS
    h = _rmsnorm(mem_ref[0], g_ref[...])
    kv = _dot(h, wkv_ref[...])
    k, v = kv[:, :d].astype(BF16), kv[:, d:].astype(BF16)
    for n in range(X_HEADS):
        cs = slice(n * hd, (n + 1) * hd)
        sw_ref[0, :, n * m:(n + 1) * m] = (_dot_nt(wq_ref[:, cs], k[:, cs]) * (hd ** -0.5)).astype(BF16)
        vw_ref[0, n * m:(n + 1) * m, :] = _dot(v[:, cs], wo_ref[cs, :]).astype(BF16)


def _mem_proj(mem, g, w_kv, w_q, w_o, *, layer):
    b, m, d = mem.shape
    hm = X_HEADS * m
    return pl.pallas_call(
        _mem_kernel,
        grid=(b,),
        in_specs=[pl.BlockSpec((1, m, d), lambda i: (i, 0, 0)),
                  _layer(g, layer),
                  _layer(w_kv, layer),
                  _layer(w_q, layer),
                  _layer(w_o, layer)],
        out_specs=[pl.BlockSpec((1, d, hm), lambda i: (i, 0, 0)),
                   pl.BlockSpec((1, hm, d), lambda i: (i, 0, 0))],
        out_shape=[jax.ShapeDtypeStruct((b, d, hm), BF16), jax.ShapeDtypeStruct((b, hm, d), BF16)],
        compiler_params=_params(1),
        name="mem_proj",
    )(mem, g, w_kv, w_q, w_o)


def _conv_kernel(x_ref, g_ref, win_ref, cw_ref, wout_ref, o_ref, carry_ref, gz_ref, *, nc):
    tm, d = x_ref.shape[1], x_ref.shape[2]

    @pl.when(pl.program_id(1) == 0)
    def _():
        carry_ref[...] = jnp.zeros_like(carry_ref)

    x = x_ref[0]
    h = _rmsnorm(x, g_ref[...]).astype(BF16)

    def project(c):
        return [_dot(h, win_ref[:, part * d + c * nc:part * d + (c + 1) * nc]) for part in range(3)]

    def gated_conv(c, gb, gc, u):
        cs = slice(c * nc, (c + 1) * nc)
        u2 = gc * u
        ext = jnp.concatenate([carry_ref[:, cs], u2], axis=0)
        s1 = pltpu.roll(ext, 1, 0)[SUBLANES:]
        s2 = pltpu.roll(ext, 2, 0)[SUBLANES:]
        z = cw_ref[0:1, cs] * s2 + cw_ref[1:2, cs] * s1 + cw_ref[2:3, cs] * u2
        carry_ref[:, cs] = u2[tm - SUBLANES:, :]
        return (gb * z).astype(BF16)

    for c in range(d // nc):
        gz_ref[:, c * nc:(c + 1) * nc] = gated_conv(c, *project(c))
    o_ref[0] = x + _dot(gz_ref[...], wout_ref[...])


def _conv_mixer(x, g, w_in, cw, w_out, *, layer, mixer, tm=1024, nc=256):
    b, s, d = x.shape
    return pl.pallas_call(
        functools.partial(_conv_kernel, nc=nc),
        grid=(b, s // tm),
        in_specs=[pl.BlockSpec((1, tm, d), lambda i, j: (i, j, 0)),
                  _layer(g, layer),
                  _layer(w_in, mixer),
                  _layer(cw, mixer),
                  _layer(w_out, mixer)],
        out_specs=pl.BlockSpec((1, tm, d), lambda i, j: (i, j, 0)),
        out_shape=jax.ShapeDtypeStruct((b, s, d), F32),
        scratch_shapes=[pltpu.VMEM((SUBLANES, d), F32), pltpu.VMEM((tm, d), BF16)],
        compiler_params=_params(2),
        name="conv_mixer",
    )(x, g, w_in, cw, w_out)


def _xattn_kernel(x_ref, g_ref, sw_ref, vw_ref, o_ref):
    tm = x_ref.shape[1]
    m = sw_ref.shape[2] // X_HEADS
    ts = tm // X_SUBTILES
    subtiles = [slice(r * ts, (r + 1) * ts) for r in range(X_SUBTILES)]
    xs = [x_ref[0, rs, :] for rs in subtiles]
    scores = [_dot(_rmsnorm(x, g_ref[...]), sw_ref[0]) for x in xs]
    for rs, x, sc_r in zip(subtiles, xs, scores):
        probs = []
        for n in range(X_HEADS):
            sc = sc_r[:, n * m:(n + 1) * m]
            e = jnp.exp(sc - jnp.max(sc, axis=-1, keepdims=True))
            probs.append((e / jnp.sum(e, axis=-1, keepdims=True)).astype(BF16))
        o_ref[0, rs, :] = x + _dot(jnp.concatenate(probs, axis=-1), vw_ref[0])


def _xattn(x, g, sw, vw, *, layer, tm=1024):
    b, s, d = x.shape
    hm = sw.shape[2]
    return pl.pallas_call(
        _xattn_kernel,
        grid=(b, s // tm),
        in_specs=[pl.BlockSpec((1, tm, d), lambda i, j: (i, j, 0)),
                  _layer(g, layer),
                  pl.BlockSpec((1, d, hm), lambda i, j: (i, 0, 0)),
                  pl.BlockSpec((1, hm, d), lambda i, j: (i, 0, 0))],
        out_specs=pl.BlockSpec((1, tm, d), lambda i, j: (i, j, 0)),
        out_shape=jax.ShapeDtypeStruct((b, s, d), F32),
        compiler_params=_params(2),
        name="xattn",
    )(x, g, sw, vw)


def _ffn_kernel(x_ref, g_ref, win_ref, wout_ref, gf_ref, o_ref, act_ref, *, nc, final_norm):
    d_ff = wout_ref.shape[0]
    x = x_ref[...]
    h = _rmsnorm(x, g_ref[...]).astype(BF16)
    for c in range(d_ff // nc):
        a = _dot(h, win_ref[:, c * nc:(c + 1) * nc])
        b = _dot(h, win_ref[:, d_ff + c * nc:d_ff + (c + 1) * nc])
        act_ref[:, c * nc:(c + 1) * nc] = ((a * _sigmoid(a)) * b).astype(BF16)
    acc = x + _dot(act_ref[...], wout_ref[...])
    if final_norm:
        acc = _rmsnorm(acc, gf_ref[...])
    o_ref[...] = acc


def _ffn(x2d, g, w_in, w_out, g_final, *, layer, final_norm, tm=512, nc=256):
    t, d = x2d.shape
    d_ff = w_out.shape[1]
    return pl.pallas_call(
        functools.partial(_ffn_kernel, nc=nc, final_norm=final_norm),
        grid=(t // tm,),
        in_specs=[pl.BlockSpec((tm, d), lambda i: (i, 0)),
                  _layer(g, layer),
                  _layer(w_in, layer),
                  _layer(w_out, layer),
                  _resident((1, d))],
        out_specs=pl.BlockSpec((tm, d), lambda i: (i, 0)),
        out_shape=jax.ShapeDtypeStruct((t, d), F32),
        scratch_shapes=[pltpu.VMEM((tm, d_ff), BF16)],
        compiler_params=_params(1),
        name="ffn",
    )(x2d, g, w_in, w_out, g_final)


SCORE_LOOKAHEAD = 2


def _split_rows(a):
    return [a[SUBLANES * i:SUBLANES * (i + 1)] for i in range(a.shape[0] // SUBLANES)]


def _decay_levels(gv):
    nv = len(gv)
    sub = lax.broadcasted_iota(jnp.int32, gv[0].shape, 0)
    e, f, t = list(gv), [jnp.ones_like(gv[0])] * nv, list(gv)
    levels = []
    for m in range(LOG2_CHUNK):
        levels.append((list(e), list(f)))
        half = 1 << m
        if half < SUBLANES:
            second = ((sub >> m) & 1) == 1
            for i in range(nv):
                t_prev = pltpu.roll(t[i], half, 0)
                t_next = pltpu.roll(t[i], SUBLANES - half, 0)
                e[i] = jnp.where(second, e[i] * t_prev, e[i])
                f[i] = jnp.where(second, f[i], f[i] * t_next)
                t[i] = t[i] * jnp.where(second, t_prev, t_next)
        else:
            hb = half // SUBLANES
            for blk in range(0, nv, 2 * hb):
                t_first, t_second = t[blk], t[blk + hb]
                both = t_first * t_second
                for i in range(blk, blk + hb):
                    f[i] = f[i] * t_second
                    e[i + hb] = e[i + hb] * t_first
                    t[i] = both
                    t[i + hb] = both
    levels.append((list(e), list(f)))
    return levels, t[0]


def _hgrn_kernel(x_ref, g_ref, win_ref, lb_ref, gn_ref, wout_ref, o_ref,
                 state_ref, q_ref, gate_ref, v_ref, sg_ref, y_ref, *, layer):
    tm, d = x_ref.shape[1], x_ref.shape[2]

    @pl.when(pl.program_id(1) == 0)
    def _():
        state_ref[...] = jnp.zeros_like(state_ref)

    lbr = lb_ref[...]
    e = jnp.exp(lbr - jnp.max(lbr, axis=0, keepdims=True))
    sm = e / jnp.sum(e, axis=0, keepdims=True)
    cum = sm[0:1, :]
    for i in range(1, layer + 1):
        cum = cum + sm[i:i + 1, :]
    lb = cum - sm[0:1, :]

    x = x_ref[0]
    h = _rmsnorm(x, g_ref[...]).astype(BF16)
    gate_ref[...] = lb + (1.0 - lb) * _sigmoid(_dot(h, win_ref[:, d:2 * d]))
    qf = _dot(h, win_ref[:, 0:d])
    q_ref[...] = qf * _sigmoid(qf)
    v_ref[...] = _dot(h, win_ref[:, 2 * d:3 * d]).astype(BF16)
    og = _dot(h, win_ref[:, 3 * d:4 * d])
    sg_ref[...] = og * _sigmoid(og)

    t_io = lax.broadcasted_iota(jnp.int32, (CHUNK, CHUNK), 0)
    s_io = lax.broadcasted_iota(jnp.int32, (CHUNK, CHUNK), 1)
    diff = t_io ^ s_io
    pair_level = jnp.zeros((CHUNK, CHUNK), jnp.int32)
    for m in range(LOG2_CHUNK):
        pair_level = jnp.where((diff >> m) == 1, m + 1, pair_level)
    level_blocks = _split_rows(jnp.where(s_io > t_io, -1, pair_level))

    def scaled(vals, factors):
        return jnp.concatenate([a * b for a, b in zip(vals, factors)], axis=0).astype(BF16)

    def scores(ci, n):
        rows = slice(ci * CHUNK, (ci + 1) * CHUNK)
        cs = slice(n * HG_DK, (n + 1) * HG_DK)
        qv = _split_rows(q_ref[rows, cs])
        gv = _split_rows(gate_ref[rows, cs])
        kv = [1.0 - g for g in gv]
        levels, chunk_decay = _decay_levels(gv)
        attn = []
        for q, g, k, lv in zip(qv, gv, kv, level_blocks):
            diag = jnp.sum(q * k, axis=-1, keepdims=True)
            pair = jnp.sum(q * g * pltpu.roll(k, 1, 0), axis=-1, keepdims=True)
            attn.append(jnp.where(lv == 0, diag, jnp.where(lv == 1, pair, 0.0)))
        for m in range(1, LOG2_CHUNK):
            e_m, f_m = levels[m]
            half_blocks = (1 << m) // SUBLANES
            if half_blocks == 0:
                t_blocks = list(range(len(qv)))
                a = _dot_nt(scaled(qv, e_m), scaled(kv, f_m))
            else:
                second = [(i // half_blocks) % 2 == 1 for i in range(len(qv))]
                t_blocks = [i for i, s in enumerate(second) if s]
                keys = [jnp.zeros_like(k) if s else k * f for k, f, s in zip(kv, f_m, second)]
                a = _dot_nt(scaled([qv[i] for i in t_blocks], [e_m[i] for i in t_blocks]),
                            jnp.concatenate(keys, axis=0))
            for a_blk, i in zip(_split_rows(a), t_blocks):
                attn[i] = jnp.where(level_blocks[i] == m + 1, a_blk, attn[i])
        e_c, f_c = levels[LOG2_CHUNK]
        return jnp.concatenate(attn, axis=0).astype(BF16), scaled(qv, e_c), scaled(kv, f_c), chunk_decay

    def recur(ci, n, attn, q_c, k_c, chunk_decay):
        rows = slice(ci * CHUNK, (ci + 1) * CHUNK)
        cs = slice(n * HG_DK, (n + 1) * HG_DK)
        vch = v_ref[rows, cs]
        st = state_ref[n]
        decay_rows = jnp.transpose(jnp.broadcast_to(chunk_decay[0:1, :], (HG_DK, HG_DK)))
        o = _dot(jnp.concatenate([q_c, attn], axis=1), jnp.concatenate([st.astype(BF16), vch], axis=0))
        state_ref[n] = st * decay_rows + _dot_tn(k_c, vch)
        on = _rmsnorm(o, gn_ref[:, cs])
        y_ref[rows, cs] = (on * sg_ref[rows, cs]).astype(BF16)

    pending = []
    for ci in range(tm // CHUNK):
        for n in range(HG_HEADS):
            pending.append((ci, n, scores(ci, n)))
            if len(pending) > SCORE_LOOKAHEAD:
                pci, pn, vals = pending.pop(0)
                recur(pci, pn, *vals)
    for pci, pn, vals in pending:
        recur(pci, pn, *vals)
    o_ref[0] = x + _dot(y_ref[...], wout_ref[...])


def _hgrn_mixer(x, g, w_in, lb_raw, g_norm, w_out, *, layer, mixer, tm=512):
    b, s, d = x.shape
    n_layers = lb_raw.shape[0]
    return pl.pallas_call(
        functools.partial(_hgrn_kernel, layer=layer),
        grid=(b, s // tm),
        in_specs=[pl.BlockSpec((1, tm, d), lambda i, j: (i, j, 0)),
                  _layer(g, layer),
                  _layer(w_in, mixer),
                  _resident((n_layers, d)),
                  _layer(g_norm, mixer),
                  _layer(w_out, mixer)],
        out_specs=pl.BlockSpec((1, tm, d), lambda i, j: (i, j, 0)),
        out_shape=jax.ShapeDtypeStruct((b, s, d), F32),
        scratch_shapes=[pltpu.VMEM((HG_HEADS, HG_DK, HG_DK), F32),
                        pltpu.VMEM((tm, d), F32),
                        pltpu.VMEM((tm, d), F32),
                        pltpu.VMEM((tm, d), BF16),
                        pltpu.VMEM((tm, d), F32),
                        pltpu.VMEM((tm, d), BF16)],
        compiler_params=_params(2),
        name="hgrn_mixer",
    )(x, g, w_in, lb_raw, g_norm, w_out)


def kernel(x, mem, norm_mix, conv_w_in, conv_w, conv_w_out, hgrn_w_in, hgrn_w_out, hgrn_norm, hgrn_lb,
           norm_xattn, norm_mem, xattn_w_q, xattn_w_kv, xattn_w_o, norm_ffn, ffn_w_in, ffn_w_out, final_norm):
    b, s, d = x.shape
    depth = norm_mix.shape[0]
    n_mixers = 2
    rows = lambda g: g.reshape(g.shape[0], 1, d)
    norm_mix, norm_xattn, norm_mem, norm_ffn, hgrn_norm = map(rows, (norm_mix, norm_xattn, norm_mem, norm_ffn,
                                                                    hgrn_norm))
    final_norm = final_norm.reshape(1, d)
    for i in range(depth):
        j = i // n_mixers
        if i % n_mixers == 0:
            x = _conv_mixer(x, norm_mix, conv_w_in, conv_w, conv_w_out, layer=i, mixer=j)
        else:
            x = _hgrn_mixer(x, norm_mix, hgrn_w_in, hgrn_lb, hgrn_norm, hgrn_w_out, layer=i, mixer=j)
        sw, vw = _mem_proj(mem, norm_mem, xattn_w_kv, xattn_w_q, xattn_w_o, layer=i)
        x = _xattn(x, norm_xattn, sw, vw, layer=i)
        x = _ffn(x.reshape(b * s, d), norm_ffn, ffn_w_in, ffn_w_out, final_norm, layer=i,
                 final_norm=(i == depth - 1)).reshape(b, s, d)
    return x
```

```python
import functools

import jax
import jax.numpy as jnp
from jax import lax
from jax.experimental import pallas as pl
from jax.experimental.pallas import tpu as pltpu

EPS = 1e-6
CHUNK = 64
LOG2_CHUNK = 6
HG_HEADS = 8
HG_DK = 128
X_HEADS = 4
X_SUBTILES = 1
CONV_W = 3
SUBLANES = 8
VMEM_LIMIT_BYTES = 56 * 1024 * 1024

F32 = jnp.float32
BF16 = jnp.bfloat16


def _rmsnorm(x, g):
    ms = jnp.mean(x * x, axis=-1, keepdims=True)
    return x * lax.rsqrt(ms + EPS) * g


def _sigmoid(x):
    return 1.0 / (1.0 + jnp.exp(-x))


def _dot(a, b):
    return jnp.dot(a.astype(BF16), b.astype(BF16), preferred_element_type=F32)


def _dot_nt(a, b):
    return lax.dot_general(a.astype(BF16), b.astype(BF16), (((1,), (1,)), ((), ())), preferred_element_type=F32)


def _dot_tn(a, b):
    return lax.dot_general(a.astype(BF16), b.astype(BF16), (((0,), (0,)), ((), ())), preferred_element_type=F32)


def _params(n_axes):
    return pltpu.CompilerParams(dimension_semantics=("arbitrary",) * n_axes,
                                vmem_limit_bytes=VMEM_LIMIT_BYTES)


def _resident(shape):
    return pl.BlockSpec(shape, lambda *_: (0,) * len(shape), pipeline_mode=pl.Buffered(1))


def _layer(stacked, layer):
    shape = stacked.shape[1:]
    return pl.BlockSpec((None,) + shape, lambda *_: (layer,) + (0,) * len(shape), pipeline_mode=pl.Buffered(1))


def _mem_kernel(mem_ref, g_ref, wkv_ref, wq_ref, wo_ref, sw_ref, vw_ref):
    m, d = mem_ref.shape[1], mem_ref.shape[2]
    hd = d // X_HEADS
    h = _rmsnorm(mem_ref[0], g_ref[...])
    kv = _dot(h, wkv_ref[...])
    k, v = kv[:, :d].astype(BF16), kv[:, d:].astype(BF16)
    for n in range(X_HEADS):
        cs = slice(n * hd, (n + 1) * hd)
        sw_ref[0, :, n * m:(n + 1) * m] = (_dot_nt(wq_ref[:, cs], k[:, cs]) * (hd ** -0.5)).astype(BF16)
        vw_ref[0, n * m:(n + 1) * m, :] = _dot(v[:, cs], wo_ref[cs, :]).astype(BF16)


def _mem_proj(mem, g, w_kv, w_q, w_o, *, layer):
    b, m, d = mem.shape
    hm = X_HEADS * m
    return pl.pallas_call(
        _mem_kernel,
        grid=(b,),
        in_specs=[pl.BlockSpec((1, m, d), lambda i: (i, 0, 0)),
                  _layer(g, layer),
                  _layer(w_kv, layer),
                  _layer(w_q, layer),
                  _layer(w_o, layer)],
        out_specs=[pl.BlockSpec((1, d, hm), lambda i: (i, 0, 0)),
                   pl.BlockSpec((1, hm, d), lambda i: (i, 0, 0))],
        out_shape=[jax.ShapeDtypeStruct((b, d, hm), BF16), jax.ShapeDtypeStruct((b, hm, d), BF16)],
        compiler_params=_params(1),
        name="mem_proj",
    )(mem, g, w_kv, w_q, w_o)


def _conv_kernel(x_ref, g_ref, win_ref, cw_ref, wout_ref, o_ref, carry_ref, gz_ref, *, nc):
    tm, d = x_ref.shape[1], x_ref.shape[2]

    @pl.when(pl.program_id(1) == 0)
    def _():
        carry_ref[...] = jnp.zeros_like(carry_ref)

    x = x_ref[0]
    h = _rmsnorm(x, g_ref[...]).astype(BF16)

    def project(c):
        return [_dot(h, win_ref[:, part * d + c * nc:part * d + (c + 1) * nc]) for part in range(3)]

    def gated_conv(c, gb, gc, u):
        cs = slice(c * nc, (c + 1) * nc)
        u2 = gc * u
        ext = jnp.concatenate([carry_ref[:, cs], u2], axis=0)
        s1 = pltpu.roll(ext, 1, 0)[SUBLANES:]
        s2 = pltpu.roll(ext, 2, 0)[SUBLANES:]
        z = cw_ref[0:1, cs] * s2 + cw_ref[1:2, cs] * s1 + cw_ref[2:3, cs] * u2
        carry_ref[:, cs] = u2[tm - SUBLANES:, :]
        return (gb * z).astype(BF16)

    for c in range(d // nc):
        gz_ref[:, c * nc:(c + 1) * nc] = gated_conv(c, *project(c))
    o_ref[0] = x + _dot(gz_ref[...], wout_ref[...])


def _conv_mixer(x, g, w_in, cw, w_out, *, layer, mixer, tm=1024, nc=256):
    b, s, d = x.shape
    return pl.pallas_call(
        functools.partial(_conv_kernel, nc=nc),
        grid=(b, s // tm),
        in_specs=[pl.BlockSpec((1, tm, d), lambda i, j: (i, j, 0)),
                  _layer(g, layer),
                  _layer(w_in, mixer),
                  _layer(cw, mixer),
                  _layer(w_out, mixer)],
        out_specs=pl.BlockSpec((1, tm, d), lambda i, j: (i, j, 0)),
        out_shape=jax.ShapeDtypeStruct((b, s, d), F32),
        scratch_shapes=[pltpu.VMEM((SUBLANES, d), F32), pltpu.VMEM((tm, d), BF16)],
        compiler_params=_params(2),
        name="conv_mixer",
    )(x, g, w_in, cw, w_out)


def _xattn_kernel(x_ref, g_ref, sw_ref, vw_ref, o_ref):
    tm = x_ref.shape[1]
    m = sw_ref.shape[2] // X_HEADS
    ts = tm // X_SUBTILES
    subtiles = [slice(r * ts, (r + 1) * ts) for r in range(X_SUBTILES)]
    xs = [x_ref[0, rs, :] for rs in subtiles]
    scores = [_dot(_rmsnorm(x, g_ref[...]), sw_ref[0]) for x in xs]
    for rs, x, sc_r in zip(subtiles, xs, scores):
        probs = []
        for n in range(X_HEADS):
            sc = sc_r[:, n * m:(n + 1) * m]
            e = jnp.exp(sc - jnp.max(sc, axis=-1, keepdims=True))
            probs.append((e / jnp.sum(e, axis=-1, keepdims=True)).astype(BF16))
        o_ref[0, rs, :] = x + _dot(jnp.concatenate(probs, axis=-1), vw_ref[0])


def _xattn(x, g, sw, vw, *, layer, tm=1024):
    b, s, d = x.shape
    hm = sw.shape[2]
    return pl.pallas_call(
        _xattn_kernel,
        grid=(b, s // tm),
        in_specs=[pl.BlockSpec((1, tm, d), lambda i, j: (i, j, 0)),
                  _layer(g, layer),
                  pl.BlockSpec((1, d, hm), lambda i, j: (i, 0, 0)),
                  pl.BlockSpec((1, hm, d), lambda i, j: (i, 0, 0))],
        out_specs=pl.BlockSpec((1, tm, d), lambda i, j: (i, j, 0)),
        out_shape=jax.ShapeDtypeStruct((b, s, d), F32),
        compiler_params=_params(2),
        name="xattn",
    )(x, g, sw, vw)


def _ffn_kernel(x_ref, g_ref, win_ref, wout_ref, gf_ref, o_ref, act_ref, *, nc, final_norm):
    d_ff = wout_ref.shape[0]
    x = x_ref[...]
    h = _rmsnorm(x, g_ref[...]).astype(BF16)
    for c in range(d_ff // nc):
        a = _dot(h, win_ref[:, c * nc:(c + 1) * nc])
        b = _dot(h, win_ref[:, d_ff + c * nc:d_ff + (c + 1) * nc])
        act_ref[:, c * nc:(c + 1) * nc] = ((a * _sigmoid(a)) * b).astype(BF16)
    acc = x + _dot(act_ref[...], wout_ref[...])
    if final_norm:
        acc = _rmsnorm(acc, gf_ref[...])
    o_ref[...] = acc


def _ffn(x2d, g, w_in, w_out, g_final, *, layer, final_norm, tm=512, nc=256):
    t, d = x2d.shape
    d_ff = w_out.shape[1]
    return pl.pallas_call(
        functools.partial(_ffn_kernel, nc=nc, final_norm=final_norm),
        grid=(t // tm,),
        in_specs=[pl.BlockSpec((tm, d), lambda i: (i, 0)),
                  _layer(g, layer),
                  _layer(w_in, layer),
                  _layer(w_out, layer),
                  _resident((1, d))],
        out_specs=pl.BlockSpec((tm, d), lambda i: (i, 0)),
        out_shape=jax.ShapeDtypeStruct((t, d), F32),
        scratch_shapes=[pltpu.VMEM((tm, d_ff), BF16)],
        compiler_params=_params(1),
        name="ffn",
    )(x2d, g, w_in, w_out, g_final)


SCORE_LOOKAHEAD = 2


def _split_rows(a):
    return [a[SUBLANES * i:SUBLANES * (i + 1)] for i in range(a.shape[0] // SUBLANES)]


def _decay_levels(gv):
    nv = len(gv)
    sub = lax.broadcasted_iota(jnp.int32, gv[0].shape, 0)
    e, f, t = list(gv), [jnp.ones_like(gv[0])] * nv, list(gv)
    levels = []
    for m in range(LOG2_CHUNK):
        levels.append((list(e), list(f)))
        half = 1 << m
        if half < SUBLANES:
            second = ((sub >> m) & 1) == 1
            for i in range(nv):
                t_prev = pltpu.roll(t[i], half, 0)
                t_next = pltpu.roll(t[i], SUBLANES - half, 0)
                e[i] = jnp.where(second, e[i] * t_prev, e[i])
                f[i] = jnp.where(second, f[i], f[i] * t_next)
                t[i] = t[i] * jnp.where(second, t_prev, t_next)
        else:
            hb = half // SUBLANES
            for blk in range(0, nv, 2 * hb):
                t_first, t_second = t[blk], t[blk + hb]
                both = t_first * t_second
                for i in range(blk, blk + hb):
                    f[i] = f[i] * t_second
                    e[i + hb] = e[i + hb] * t_first
                    t[i] = both
                    t[i + hb] = both
    levels.append((list(e), list(f)))
    return levels, t[0]


def _hgrn_kernel(x_ref, g_ref, win_ref, lb_ref, gn_ref, wout_ref, o_ref,
                 state_ref, q_ref, gate_ref, v_ref, sg_ref, y_ref, *, layer):
    tm, d = x_ref.shape[1], x_ref.shape[2]

    @pl.when(pl.program_id(1) == 0)
    def _():
        state_ref[...] = jnp.zeros_like(state_ref)

    lbr = lb_ref[...]
    e = jnp.exp(lbr - jnp.max(lbr, axis=0, keepdims=True))
    sm = e / jnp.sum(e, axis=0, keepdims=True)
    cum = sm[0:1, :]
    for i in range(1, layer + 1):
        cum = cum + sm[i:i + 1, :]
    lb = cum - sm[0:1, :]

    x = x_ref[0]
    h = _rmsnorm(x, g_ref[...]).astype(BF16)
    gate_ref[...] = lb + (1.0 - lb) * _sigmoid(_dot(h, win_ref[:, d:2 * d]))
    qf = _dot(h, win_ref[:, 0:d])
    q_ref[...] = qf * _sigmoid(qf)
    v_ref[...] = _dot(h, win_ref[:, 2 * d:3 * d]).astype(BF16)
    og = _dot(h, win_ref[:, 3 * d:4 * d])
    sg_ref[...] = og * _sigmoid(og)

    t_io = lax.broadcasted_iota(jnp.int32, (CHUNK, CHUNK), 0)
    s_io = lax.broadcasted_iota(jnp.int32, (CHUNK, CHUNK), 1)
    diff = t_io ^ s_io
    pair_level = jnp.zeros((CHUNK, CHUNK), jnp.int32)
    for m in range(LOG2_CHUNK):
        pair_level = jnp.where((diff >> m) == 1, m + 1, pair_level)
    level_blocks = _split_rows(jnp.where(s_io > t_io, -1, pair_level))

    def scaled(vals, factors):
        return jnp.concatenate([a * b for a, b in zip(vals, factors)], axis=0).astype(BF16)

    def scores(ci, n):
        rows = slice(ci * CHUNK, (ci + 1) * CHUNK)
        cs = slice(n * HG_DK, (n + 1) * HG_DK)
        qv = _split_rows(q_ref[rows, cs])
        gv = _split_rows(gate_ref[rows, cs])
        kv = [1.0 - g for g in gv]
        levels, chunk_decay = _decay_levels(gv)
        attn = []
        for q, g, k, lv in zip(qv, gv, kv, level_blocks):
            diag = jnp.sum(q * k, axis=-1, keepdims=True)
            pair = jnp.sum(q * g * pltpu.roll(k, 1, 0), axis=-1, keepdims=True)
            attn.append(jnp.where(lv == 0, diag, jnp.where(lv == 1, pair, 0.0)))
        for m in range(1, LOG2_CHUNK):
            e_m, f_m = levels[m]
            half_blocks = (1 << m) // SUBLANES
            if half_blocks == 0:
                t_blocks = list(range(len(qv)))
                a = _dot_nt(scaled(qv, e_m), scaled(kv, f_m))
            else:
                second = [(i // half_blocks) % 2 == 1 for i in range(len(qv))]
                t_blocks = [i for i, s in enumerate(second) if s]
                keys = [jnp.zeros_like(k) if s else k * f for k, f, s in zip(kv, f_m, second)]
                a = _dot_nt(scaled([qv[i] for i in t_blocks], [e_m[i] for i in t_blocks]),
                            jnp.concatenate(keys, axis=0))
            for a_blk, i in zip(_split_rows(a), t_blocks):
                attn[i] = jnp.where(level_blocks[i] == m + 1, a_blk, attn[i])
        e_c, f_c = levels[LOG2_CHUNK]
        return jnp.concatenate(attn, axis=0).astype(BF16), scaled(qv, e_c), scaled(kv, f_c), chunk_decay

    def recur(ci, n, attn, q_c, k_c, chunk_decay):
        rows = slice(ci * CHUNK, (ci + 1) * CHUNK)
        cs = slice(n * HG_DK, (n + 1) * HG_DK)
        vch = v_ref[rows, cs]
        st = state_ref[n]
        decay_rows = jnp.transpose(jnp.broadcast_to(chunk_decay[0:1, :], (HG_DK, HG_DK)))
        o = _dot(jnp.concatenate([q_c, attn], axis=1), jnp.concatenate([st.astype(BF16), vch], axis=0))
        state_ref[n] = st * decay_rows + _dot_tn(k_c, vch)
        on = _rmsnorm(o, gn_ref[:, cs])
        y_ref[rows, cs] = (on * sg_ref[rows, cs]).astype(BF16)

    pending = []
    for ci in range(tm // CHUNK):
        for n in range(HG_HEADS):
            pending.append((ci, n, scores(ci, n)))
            if len(pending) > SCORE_LOOKAHEAD:
                pci, pn, vals = pending.pop(0)
                recur(pci, pn, *vals)
    for pci, pn, vals in pending:
        recur(pci, pn, *vals)
    o_ref[0] = x + _dot(y_ref[...], wout_ref[...])


def _hgrn_mixer(x, g, w_in, lb_raw, g_norm, w_out, *, layer, mixer, tm=512):
    b, s, d = x.shape
    n_layers = lb_raw.shape[0]
    return pl.pallas_call(
        functools.partial(_hgrn_kernel, layer=layer),
        grid=(b, s // tm),
        in_specs=[pl.BlockSpec((1, tm, d), lambda i, j: (i, j, 0)),
                  _layer(g, layer),
                  _layer(w_in, mixer),
                  _resident((n_layers, d)),
                  _layer(g_norm, mixer),
                  _layer(w_out, mixer)],
        out_specs=pl.BlockSpec((1, tm, d), lambda i, j: (i, j, 0)),
        out_shape=jax.ShapeDtypeStruct((b, s, d), F32),
        scratch_shapes=[pltpu.VMEM((HG_HEADS, HG_DK, HG_DK), F32),
                        pltpu.VMEM((tm, d), F32),
                        pltpu.VMEM((tm, d), F32),
                        pltpu.VMEM((tm, d), BF16),
                        pltpu.VMEM((tm, d), F32),
                        pltpu.VMEM((tm, d), BF16)],
        compiler_params=_params(2),
        name="hgrn_mixer",
    )(x, g, w_in, lb_raw, g_norm, w_out)


def kernel(x, mem, norm_mix, conv_w_in, conv_w, conv_w_out, hgrn_w_in, hgrn_w_out, hgrn_norm, hgrn_lb,
           norm_xattn, norm_mem, xattn_w_q, xattn_w_kv, xattn_w_o, norm_ffn, ffn_w_in, ffn_w_out, final_norm):
    b, s, d = x.shape
    depth = norm_mix.shape[0]
    n_mixers = 2
    rows = lambda g: g.reshape(g.shape[0], 1, d)
    norm_mix, norm_xattn, norm_mem, norm_ffn, hgrn_norm = map(rows, (norm_mix, norm_xattn, norm_mem, norm_ffn,
                                                                    hgrn_norm))
    final_norm = final_norm.reshape(1, d)
    for i in range(depth):
        j = i // n_mixers
        if i % n_mixers == 0:
            x = _conv_mixer(x, norm_mix, conv_w_in, conv_w, conv_w_out, layer=i, mixer=j)
        else:
            x = _hgrn_mixer(x, norm_mix, hgrn_w_in, hgrn_lb, hgrn_norm, hgrn_w_out, layer=i, mixer=j)
        sw, vw = _mem_proj(mem, norm_mem, xattn_w_kv, xattn_w_q, xattn_w_o, layer=i)
        x = _xattn(x, norm_xattn, sw, vw, layer=i)
        x = _ffn(x.reshape(b * s, d), norm_ffn, ffn_w_in, ffn_w_out, final_norm, layer=i,
                 final_norm=(i == depth - 1)).reshape(b, s, d)
    return x
```

```python
import functools

import jax
import jax.numpy as jnp
from jax import lax
from jax.experimental import pallas as pl
from jax.experimental.pallas import tpu as pltpu

EPS = 1e-6
CHUNK = 64
LOG2_CHUNK = 6
HG_HEADS = 8
HG_DK = 128
X_HEADS = 4
MEM_BATCHES = 2
X_SUBTILES = 1
CONV_W = 3
SUBLANES = 8
VMEM_LIMIT_BYTES = 56 * 1024 * 1024

F32 = jnp.float32
BF16 = jnp.bfloat16


def _rmsnorm(x, g):
    ms = jnp.mean(x * x, axis=-1, keepdims=True)
    return x * lax.rsqrt(ms + EPS) * g


def _sigmoid(x):
    return 1.0 / (1.0 + jnp.exp(-x))


def _dot(a, b):
    return jnp.dot(a.astype(BF16), b.astype(BF16), preferred_element_type=F32)


def _dot_nt(a, b):
    return lax.dot_general(a.astype(BF16), b.astype(BF16), (((1,), (1,)), ((), ())), preferred_element_type=F32)


def _dot_tn(a, b):
    return lax.dot_general(a.astype(BF16), b.astype(BF16), (((0,), (0,)), ((), ())), preferred_element_type=F32)


def _params(n_axes):
    return pltpu.CompilerParams(dimension_semantics=("arbitrary",) * n_axes,
                                vmem_limit_bytes=VMEM_LIMIT_BYTES)


def _resident(shape):
    return pl.BlockSpec(shape, lambda *_: (0,) * len(shape), pipeline_mode=pl.Buffered(1))


def _layer(stacked, layer):
    shape = stacked.shape[1:]
    return pl.BlockSpec((None,) + shape, lambda *_: (layer,) + (0,) * len(shape), pipeline_mode=pl.Buffered(1))


def _mem_kernel(mem_ref, g_ref, wkv_ref, wq_ref, wo_ref, sw_ref, vw_ref):
    nb, m, d = mem_ref.shape
    hd = d // X_HEADS
    h = _rmsnorm(mem_ref[...].reshape(nb * m, d), g_ref[...])
    kv = _dot(h, wkv_ref[...])
    k, v = kv[:, :d].astype(BF16), kv[:, d:].astype(BF16)
    for i in range(nb):
        rows = slice(i * m, (i + 1) * m)
        for n in range(X_HEADS):
            cs = slice(n * hd, (n + 1) * hd)
            sw_ref[i, :, n * m:(n + 1) * m] = (_dot_nt(wq_ref[:, cs], k[rows, cs]) * (hd ** -0.5)).astype(BF16)
            vw_ref[i, n * m:(n + 1) * m, :] = _dot(v[rows, cs], wo_ref[cs, :]).astype(BF16)


def _mem_proj(mem, g, w_kv, w_q, w_o, *, layer):
    b, m, d = mem.shape
    hm = X_HEADS * m
    return pl.pallas_call(
        _mem_kernel,
        grid=(b // MEM_BATCHES,),
        in_specs=[pl.BlockSpec((MEM_BATCHES, m, d), lambda i: (i, 0, 0)),
                  _layer(g, layer),
                  _layer(w_kv, layer),
                  _layer(w_q, layer),
                  _layer(w_o, layer)],
        out_specs=[pl.BlockSpec((MEM_BATCHES, d, hm), lambda i: (i, 0, 0)),
                   pl.BlockSpec((MEM_BATCHES, hm, d), lambda i: (i, 0, 0))],
        out_shape=[jax.ShapeDtypeStruct((b, d, hm), BF16), jax.ShapeDtypeStruct((b, hm, d), BF16)],
        compiler_params=_params(1),
        name="mem_proj",
    )(mem, g, w_kv, w_q, w_o)


def _conv_kernel(x_ref, g_ref, win_ref, cw_ref, wout_ref, o_ref, carry_ref, gz_ref, *, nc):
    tm, d = x_ref.shape[1], x_ref.shape[2]

    @pl.when(pl.program_id(1) == 0)
    def _():
        carry_ref[...] = jnp.zeros_like(carry_ref)

    x = x_ref[0]
    h = _rmsnorm(x, g_ref[...]).astype(BF16)

    def project(c):
        return [_dot(h, win_ref[:, part * d + c * nc:part * d + (c + 1) * nc]) for part in range(3)]

    def gated_conv(c, gb, gc, u):
        cs = slice(c * nc, (c + 1) * nc)
        u2 = gc * u
        ext = jnp.concatenate([carry_ref[:, cs], u2], axis=0)
        s1 = pltpu.roll(ext, 1, 0)[SUBLANES:]
        s2 = pltpu.roll(ext, 2, 0)[SUBLANES:]
        z = cw_ref[0:1, cs] * s2 + cw_ref[1:2, cs] * s1 + cw_ref[2:3, cs] * u2
        carry_ref[:, cs] = u2[tm - SUBLANES:, :]
        return (gb * z).astype(BF16)

    for c in range(d // nc):
        gz_ref[:, c * nc:(c + 1) * nc] = gated_conv(c, *project(c))
    o_ref[0] = x + _dot(gz_ref[...], wout_ref[...])


def _conv_mixer(x, g, w_in, cw, w_out, *, layer, mixer, tm=1024, nc=256):
    b, s, d = x.shape
    return pl.pallas_call(
        functools.partial(_conv_kernel, nc=nc),
        grid=(b, s // tm),
        in_specs=[pl.BlockSpec((1, tm, d), lambda i, j: (i, j, 0)),
                  _layer(g, layer),
                  _layer(w_in, mixer),
                  _layer(cw, mixer),
                  _layer(w_out, mixer)],
        out_specs=pl.BlockSpec((1, tm, d), lambda i, j: (i, j, 0)),
        out_shape=jax.ShapeDtypeStruct((b, s, d), F32),
        scratch_shapes=[pltpu.VMEM((SUBLANES, d), F32), pltpu.VMEM((tm, d), BF16)],
        compiler_params=_params(2),
        name="conv_mixer",
    )(x, g, w_in, cw, w_out)


def _xattn_kernel(x_ref, g_ref, sw_ref, vw_ref, o_ref):
    tm = x_ref.shape[1]
    m = sw_ref.shape[2] // X_HEADS
    ts = tm // X_SUBTILES
    subtiles = [slice(r * ts, (r + 1) * ts) for r in range(X_SUBTILES)]
    xs = [x_ref[0, rs, :] for rs in subtiles]
    scores = [_dot(_rmsnorm(x, g_ref[...]), sw_ref[0]) for x in xs]
    for rs, x, sc_r in zip(subtiles, xs, scores):
        probs = []
        for n in range(X_HEADS):
            sc = sc_r[:, n * m:(n + 1) * m]
            e = jnp.exp(sc - jnp.max(sc, axis=-1, keepdims=True))
            probs.append((e / jnp.sum(e, axis=-1, keepdims=True)).astype(BF16))
        o_ref[0, rs, :] = x + _dot(jnp.concatenate(probs, axis=-1), vw_ref[0])


def _xattn(x, g, sw, vw, *, layer, tm=1024):
    b, s, d = x.shape
    hm = sw.shape[2]
    return pl.pallas_call(
        _xattn_kernel,
        grid=(b, s // tm),
        in_specs=[pl.BlockSpec((1, tm, d), lambda i, j: (i, j, 0)),
                  _layer(g, layer),
                  pl.BlockSpec((1, d, hm), lambda i, j: (i, 0, 0)),
                  pl.BlockSpec((1, hm, d), lambda i, j: (i, 0, 0))],
        out_specs=pl.BlockSpec((1, tm, d), lambda i, j: (i, j, 0)),
        out_shape=jax.ShapeDtypeStruct((b, s, d), F32),
        compiler_params=_params(2),
        name="xattn",
    )(x, g, sw, vw)


def _ffn_kernel(x_ref, g_ref, win_ref, wout_ref, gf_ref, o_ref, act_ref, *, nc, final_norm):
    d_ff = wout_ref.shape[0]
    x = x_ref[...]
    h = _rmsnorm(x, g_ref[...]).astype(BF16)
    for c in range(d_ff // nc):
        a = _dot(h, win_ref[:, c * nc:(c + 1) * nc])
        b = _dot(h, win_ref[:, d_ff + c * nc:d_ff + (c + 1) * nc])
        act_ref[:, c * nc:(c + 1) * nc] = ((a * _sigmoid(a)) * b).astype(BF16)
    acc = x + _dot(act_ref[...], wout_ref[...])
    if final_norm:
        acc = _rmsnorm(acc, gf_ref[...])
    o_ref[...] = acc


def _ffn(x2d, g, w_in, w_out, g_final, *, layer, final_norm, tm=512, nc=256):
    t, d = x2d.shape
    d_ff = w_out.shape[1]
    return pl.pallas_call(
        functools.partial(_ffn_kernel, nc=nc, final_norm=final_norm),
        grid=(t // tm,),
        in_specs=[pl.BlockSpec((tm, d), lambda i: (i, 0)),
                  _layer(g, layer),
                  _layer(w_in, layer),
                  _layer(w_out, layer),
                  _resident((1, d))],
        out_specs=pl.BlockSpec((tm, d), lambda i: (i, 0)),
        out_shape=jax.ShapeDtypeStruct((t, d), F32),
        scratch_shapes=[pltpu.VMEM((tm, d_ff), BF16)],
        compiler_params=_params(1),
        name="ffn",
    )(x2d, g, w_in, w_out, g_final)


SCORE_LOOKAHEAD = 2


def _split_rows(a):
    return [a[SUBLANES * i:SUBLANES * (i + 1)] for i in range(a.shape[0] // SUBLANES)]


def _decay_levels(gv):
    nv = len(gv)
    sub = lax.broadcasted_iota(jnp.int32, gv[0].shape, 0)
    e, f, t = list(gv), [jnp.ones_like(gv[0])] * nv, list(gv)
    levels = []
    for m in range(LOG2_CHUNK):
        levels.append((list(e), list(f)))
        half = 1 << m
        if half < SUBLANES:
            second = ((sub >> m) & 1) == 1
            for i in range(nv):
                t_prev = pltpu.roll(t[i], half, 0)
                t_next = pltpu.roll(t[i], SUBLANES - half, 0)
                e[i] = jnp.where(second, e[i] * t_prev, e[i])
                f[i] = jnp.where(second, f[i], f[i] * t_next)
                t[i] = t[i] * jnp.where(second, t_prev, t_next)
        else:
            hb = half // SUBLANES
            for blk in range(0, nv, 2 * hb):
                t_first, t_second = t[blk], t[blk + hb]
                both = t_first * t_second
                for i in range(blk, blk + hb):
                    f[i] = f[i] * t_second
                    e[i + hb] = e[i + hb] * t_first
                    t[i] = both
                    t[i + hb] = both
    levels.append((list(e), list(f)))
    return levels, t[0]


def _hgrn_kernel(x_ref, g_ref, win_ref, lb_ref, gn_ref, wout_ref, o_ref,
                 state_ref, q_ref, gate_ref, v_ref, sg_ref, y_ref, *, layer):
    tm, d = x_ref.shape[1], x_ref.shape[2]

    @pl.when(pl.program_id(1) == 0)
    def _():
        state_ref[...] = jnp.zeros_like(state_ref)

    lbr = lb_ref[...]
    e = jnp.exp(lbr - jnp.max(lbr, axis=0, keepdims=True))
    sm = e / jnp.sum(e, axis=0, keepdims=True)
    cum = sm[0:1, :]
    for i in range(1, layer + 1):
        cum = cum + sm[i:i + 1, :]
    lb = cum - sm[0:1, :]

    x = x_ref[0]
    h = _rmsnorm(x, g_ref[...]).astype(BF16)
    gate_ref[...] = lb + (1.0 - lb) * _sigmoid(_dot(h, win_ref[:, d:2 * d]))
    qf = _dot(h, win_ref[:, 0:d])
    q_ref[...] = qf * _sigmoid(qf)
    v_ref[...] = _dot(h, win_ref[:, 2 * d:3 * d]).astype(BF16)
    og = _dot(h, win_ref[:, 3 * d:4 * d])
    sg_ref[...] = og * _sigmoid(og)

    t_io = lax.broadcasted_iota(jnp.int32, (CHUNK, CHUNK), 0)
    s_io = lax.broadcasted_iota(jnp.int32, (CHUNK, CHUNK), 1)
    diff = t_io ^ s_io
    pair_level = jnp.zeros((CHUNK, CHUNK), jnp.int32)
    for m in range(LOG2_CHUNK):
        pair_level = jnp.where((diff >> m) == 1, m + 1, pair_level)
    level_blocks = _split_rows(jnp.where(s_io > t_io, -1, pair_level))

    def scaled(vals, factors):
        return jnp.concatenate([a * b for a, b in zip(vals, factors)], axis=0).astype(BF16)

    def scores(ci, n):
        rows = slice(ci * CHUNK, (ci + 1) * CHUNK)
        cs = slice(n * HG_DK, (n + 1) * HG_DK)
        qv = _split_rows(q_ref[rows, cs])
        gv = _split_rows(gate_ref[rows, cs])
        kv = [1.0 - g for g in gv]
        levels, chunk_decay = _decay_levels(gv)
        attn = []
        for q, g, k, lv in zip(qv, gv, kv, level_blocks):
            diag = jnp.sum(q * k, axis=-1, keepdims=True)
            pair = jnp.sum(q * g * pltpu.roll(k, 1, 0), axis=-1, keepdims=True)
            attn.append(jnp.where(lv == 0, diag, jnp.where(lv == 1, pair, 0.0)))
        for m in range(1, LOG2_CHUNK):
            e_m, f_m = levels[m]
            half_blocks = (1 << m) // SUBLANES
            if half_blocks == 0:
                t_blocks = list(range(len(qv)))
                a = _dot_nt(scaled(qv, e_m), scaled(kv, f_m))
            else:
                second = [(i // half_blocks) % 2 == 1 for i in range(len(qv))]
                t_blocks = [i for i, s in enumerate(second) if s]
                keys = [jnp.zeros_like(k) if s else k * f for k, f, s in zip(kv, f_m, second)]
                a = _dot_nt(scaled([qv[i] for i in t_blocks], [e_m[i] for i in t_blocks]),
                            jnp.concatenate(keys, axis=0))
            for a_blk, i in zip(_split_rows(a), t_blocks):
                attn[i] = jnp.where(level_blocks[i] == m + 1, a_blk, attn[i])
        e_c, f_c = levels[LOG2_CHUNK]
        return jnp.concatenate(attn, axis=0).astype(BF16), scaled(qv, e_c), scaled(kv, f_c), chunk_decay

    def recur(ci, n, attn, q_c, k_c, chunk_decay):
        rows = slice(ci * CHUNK, (ci + 1) * CHUNK)
        cs = slice(n * HG_DK, (n + 1) * HG_DK)
        vch = v_ref[rows, cs]
        st = state_ref[n]
        decay_rows = jnp.transpose(jnp.broadcast_to(chunk_decay[0:1, :], (HG_DK, HG_DK)))
        o = _dot(jnp.concatenate([q_c, attn], axis=1), jnp.concatenate([st.astype(BF16), vch], axis=0))
        state_ref[n] = st * decay_rows + _dot_tn(k_c, vch)
        on = _rmsnorm(o, gn_ref[:, cs])
        y_ref[rows, cs] = (on * sg_ref[rows, cs]).astype(BF16)

    pending = []
    for ci in range(tm // CHUNK):
        for n in range(HG_HEADS):
            pending.append((ci, n, scores(ci, n)))
            if len(pending) > SCORE_LOOKAHEAD:
                pci, pn, vals = pending.pop(0)
                recur(pci, pn, *vals)
    for pci, pn, vals in pending:
        recur(pci, pn, *vals)
    o_ref[0] = x + _dot(y_ref[...], wout_ref[...])


def _hgrn_mixer(x, g, w_in, lb_raw, g_norm, w_out, *, layer, mixer, tm=512):
    b, s, d = x.shape
    n_layers = lb_raw.shape[0]
    return pl.pallas_call(
        functools.partial(_hgrn_kernel, layer=layer),
        grid=(b, s // tm),
        in_specs=[pl.BlockSpec((1, tm, d), lambda i, j: (i, j, 0)),
                  _layer(g, layer),
                  _layer(w_in, mixer),
                  _resident((n_layers, d)),
                  _layer(g_norm, mixer),
                  _layer(w_out, mixer)],
        out_specs=pl.BlockSpec((1, tm, d), lambda i, j: (i, j, 0)),
        out_shape=jax.ShapeDtypeStruct((b, s, d), F32),
        scratch_shapes=[pltpu.VMEM((HG_HEADS, HG_DK, HG_DK), F32),
                        pltpu.VMEM((tm, d), F32),
                        pltpu.VMEM((tm, d), F32),
                        pltpu.VMEM((tm, d), BF16),
                        pltpu.VMEM((tm, d), F32),
                        pltpu.VMEM((tm, d), BF16)],
        compiler_params=_params(2),
        name="hgrn_mixer",
    )(x, g, w_in, lb_raw, g_norm, w_out)


def kernel(x, mem, norm_mix, conv_w_in, conv_w, conv_w_out, hgrn_w_in, hgrn_w_out, hgrn_norm, hgrn_lb,
           norm_xattn, norm_mem, xattn_w_q, xattn_w_kv, xattn_w_o, norm_ffn, ffn_w_in, ffn_w_out, final_norm):
    b, s, d = x.shape
    depth = norm_mix.shape[0]
    n_mixers = 2
    rows = lambda g: g.reshape(g.shape[0], 1, d)
    norm_mix, norm_xattn, norm_mem, norm_ffn, hgrn_norm = map(rows, (norm_mix, norm_xattn, norm_mem, norm_ffn,
                                                                    hgrn_norm))
    final_norm = final_norm.reshape(1, d)
    for i in range(depth):
        j = i // n_mixers
        if i % n_mixers == 0:
            x = _conv_mixer(x, norm_mix, conv_w_in, conv_w, conv_w_out, layer=i, mixer=j)
        else:
            x = _hgrn_mixer(x, norm_mix, hgrn_w_in, hgrn_lb, hgrn_norm, hgrn_w_out, layer=i, mixer=j)
        sw, vw = _mem_proj(mem, norm_mem, xattn_w_kv, xattn_w_q, xattn_w_o, layer=i)
        x = _xattn(x, norm_xattn, sw, vw, layer=i)
        x = _ffn(x.reshape(b * s, d), norm_ffn, ffn_w_in, ffn_w_out, final_norm, layer=i,
                 final_norm=(i == depth - 1)).reshape(b, s, d)
    return x
```
